```python
import math
import jax, jax.numpy as jnp
from jax import lax
import numpy as np

D_MODEL = 4096
BATCH = 4
SEQ = 2048
DEPTH = 1

D_FF = 11008
ATTN_HEAD_DIM = 128
N_ATTN_HEADS = D_MODEL // (2 * ATTN_HEAD_DIM)
D_ATTN = N_ATTN_HEADS * ATTN_HEAD_DIM
DILATED_CONFIGS = ((128, 1), (512, 4), (2048, 16))
ATTN_BLOCK = 128
DN_HEAD_DIM = 128
N_DN_HEADS = D_MODEL // (2 * DN_HEAD_DIM)
D_DN = N_DN_HEADS * DN_HEAD_DIM
CONV_WIDTH = 4
CHUNK = 64
D_MIX = D_ATTN + D_DN
IN_SPLITS = (D_ATTN, D_ATTN, D_ATTN, 3 * D_DN, D_DN, N_DN_HEADS, N_DN_HEADS)
D_IN_PROJ = sum(IN_SPLITS)
EPS = 1e-6

kernel_name = "hymba_dilated_swa_gated_deltanet_macaron"


def _rmsnorm(x, w):
    xf = x.astype(jnp.float32)
    y = xf * lax.rsqrt(jnp.mean(xf * xf, axis=-1, keepdims=True) + EPS)
    return (y * w.astype(jnp.float32)).astype(x.dtype)


def _swiglu(h, w_gate, w_up, w_down):
    return (jax.nn.silu(h @ w_gate) * (h @ w_up)) @ w_down


def _band_attention(q, k, v, steps):
    G, L, H, Dh = q.shape
    nb = -(-L // ATTN_BLOCK)
    lp = nb * ATTN_BLOCK
    qb = jnp.pad(q, ((0, 0), (0, lp - L), (0, 0), (0, 0))).reshape(G, nb, ATTN_BLOCK, H, Dh)

    def band(t):
        t = jnp.pad(t, ((0, 0), (ATTN_BLOCK, lp - L), (0, 0), (0, 0)))
        t = t.reshape(G, nb + 1, ATTN_BLOCK, H, Dh)
        return jnp.concatenate([t[:, :-1], t[:, 1:]], axis=2)

    kw, vw = band(k), band(v)
    s = jnp.einsum('gnqhd,gnkhd->gnhqk', qb, kw, preferred_element_type=jnp.float32) * (Dh ** -0.5)
    i = jnp.arange(ATTN_BLOCK)[:, None]
    j = jnp.arange(2 * ATTN_BLOCK)[None, :]
    dist = i + ATTN_BLOCK - j
    kpos = jnp.arange(nb)[:, None, None] * ATTN_BLOCK - ATTN_BLOCK + j
    valid = (dist >= 0) & (dist <= steps) & (kpos >= 0)
    s = jnp.where(valid[:, None], s, -jnp.inf)
    m = jnp.max(s, axis=-1, keepdims=True)
    p = jnp.exp(s - m)
    den = jnp.sum(p, axis=-1)
    num = jnp.einsum('gnhqk,gnkhd->gnqhd', p, vw.astype(jnp.float32))
    m = m[..., 0].transpose(0, 1, 3, 2).reshape(G, lp, H)[:, :L]
    den = den.transpose(0, 1, 3, 2).reshape(G, lp, H)[:, :L]
    num = num.reshape(G, lp, H, Dh)[:, :L]
    return m, num, den


def _dilated_attention(q, k, v):
    B, S, H, Dh = q.shape
    ms, nums, dens = [], [], []
    for window, d in DILATED_CONFIGS:
        L = S // d

        def to_res(t):
            return t.reshape(B, L, d, H, Dh).transpose(0, 2, 1, 3, 4).reshape(B * d, L, H, Dh)

        def from_res(t):
            rest = t.shape[2:]
            return jnp.swapaxes(t.reshape(B, d, L, *rest), 1, 2).reshape(B, S, *rest)

        m, num, den = _band_attention(to_res(q), to_res(k), to_res(v), window // d)
        ms.append(from_res(m)); nums.append(from_res(num)); dens.append(from_res(den))
    m_all = jnp.stack(ms)
    w = jnp.exp(m_all - jnp.max(m_all, axis=0, keepdims=True))
    num = jnp.sum(w[..., None] * jnp.stack(nums), axis=0)
    den = jnp.sum(w * jnp.stack(dens), axis=0)
    return num / den[..., None]


def _causal_conv(x, w):
    S = x.shape[1]
    xp = jnp.pad(x, ((0, 0), (CONV_WIDTH - 1, 0), (0, 0)))
    return sum(w[i] * xp[:, i:i + S] for i in range(CONV_WIDTH))


def _gated_delta_rule(q, k, v, g, beta):
    B, S, H, Dk = q.shape
    Dv = v.shape[-1]
    N = S // CHUNK

    def chunks(t):
        return jnp.swapaxes(t.reshape(B, N, CHUNK, H, *t.shape[3:]), 2, 3)

    q, k, v, g, beta = (chunks(t.astype(jnp.float32)) for t in (q, k, v, g, beta))
    gc = jnp.cumsum(g, axis=-1)
    idx = jnp.arange(CHUNK)
    incl = idx[:, None] >= idx[None, :]
    strict = idx[:, None] > idx[None, :]
    decay = jnp.exp(jnp.where(incl, gc[..., :, None] - gc[..., None, :], -jnp.inf))
    kb = k * beta[..., None]
    kk = jnp.einsum('bnhid,bnhjd->bnhij', kb, k)
    a = jnp.where(strict, kk * decay, 0.0) + jnp.eye(CHUNK, dtype=jnp.float32)
    rhs = jnp.concatenate([kb * jnp.exp(gc)[..., None], v * beta[..., None]], axis=-1)
    sol = lax.linalg.triangular_solve(a, rhs, left_side=True, lower=True, unit_diagonal=True)
    w_c, u_c = sol[..., :Dk], sol[..., Dk:]
    qk = jnp.einsum('bnhid,bnhjd->bnhij', q, k) * decay
    q_dec = q * jnp.exp(gc)[..., None]
    k_dec = k * jnp.exp(gc[..., -1:] - gc)[..., None]
    g_last = jnp.exp(gc[..., -1])

    def step(state, xs):
        wc, uc, qkc, qdc, kdc, glc = xs
        v_new = uc - jnp.einsum('bhcd,bhdv->bhcv', wc, state)
        o = jnp.einsum('bhcd,bhdv->bhcv', qdc, state) + jnp.einsum('bhij,bhjv->bhiv', qkc, v_new)
        state = state * glc[..., None, None] + jnp.einsum('bhcd,bhcv->bhdv', kdc, v_new)
        return state, o

    xs = tuple(jnp.moveaxis(t, 1, 0) for t in (w_c, u_c, qk, q_dec, k_dec, g_last))
    state0 = jnp.zeros((B, H, Dk, Dv), jnp.float32)
    _, o = lax.scan(step, state0, xs)
    return o.transpose(1, 0, 3, 2, 4).reshape(B, S, H, Dv)


def _hybrid_mixer(h, w_in, conv_w, a_log, dt_bias, dn_norm, w_out):
    B, S, _ = h.shape
    proj = h @ w_in
    cuts = [int(c) for c in np.cumsum(IN_SPLITS)[:-1]]
    aq, ak, av, dqkv, dz, db, da = jnp.split(proj, cuts, axis=-1)
    heads_a = lambda t: t.reshape(B, S, N_ATTN_HEADS, ATTN_HEAD_DIM)
    attn = _dilated_attention(heads_a(aq), heads_a(ak), heads_a(av))
    attn = attn.reshape(B, S, D_ATTN).astype(h.dtype)
    dqkv = jax.nn.silu(_causal_conv(dqkv, conv_w))
    dq, dk, dv = jnp.split(dqkv, 3, axis=-1)
    heads_b = lambda t: t.reshape(B, S, N_DN_HEADS, DN_HEAD_DIM).astype(jnp.float32)
    dq, dk, dv = heads_b(dq), heads_b(dk), heads_b(dv)
    l2 = lambda t: t * lax.rsqrt(jnp.sum(t * t, axis=-1, keepdims=True) + EPS)
    dq = l2(dq) * (DN_HEAD_DIM ** -0.5)
    dk = l2(dk)
    beta = jax.nn.sigmoid(db.astype(jnp.float32))
    g = -jnp.exp(a_log.astype(jnp.float32)) * jax.nn.softplus(da.astype(jnp.float32) + dt_bias.astype(jnp.float32))
    o = _gated_delta_rule(dq, dk, dv, g, beta)
    o = o * lax.rsqrt(jnp.mean(o * o, axis=-1, keepdims=True) + EPS) * dn_norm.astype(jnp.float32)
    o = o * jax.nn.silu(heads_b(dz))
    dn = o.reshape(B, S, D_DN).astype(h.dtype)
    return jnp.concatenate([attn, dn], axis=-1) @ w_out


def setup_inputs(seed: int = 0) -> dict:
    key = jax.random.key(seed)
    ks = jax.random.split(key, 20)
    f32 = jnp.float32
    nrm = lambda k, shape, fan_in: jax.random.normal(k, shape, f32) * fan_in ** -0.5
    gain = lambda k, n: 1.0 + 0.05 * jax.random.normal(k, (n,), f32)
    dt = jnp.exp(jax.random.uniform(ks[9], (N_DN_HEADS,), f32, math.log(1e-3), math.log(1e-1)))
    return {
        "x": jax.random.normal(ks[0], (BATCH, SEQ, D_MODEL), f32),
        "ffn1_norm": gain(ks[1], D_MODEL),
        "ffn1_w_gate": nrm(ks[2], (D_MODEL, D_FF), D_MODEL),
        "ffn1_w_up": nrm(ks[3], (D_MODEL, D_FF), D_MODEL),
        "ffn1_w_down": nrm(ks[4], (D_FF, D_MODEL), D_FF),
        "mix_norm": gain(ks[5], D_MODEL),
        "w_in": nrm(ks[6], (D_MODEL, D_IN_PROJ), D_MODEL),
        "conv_w": nrm(ks[7], (CONV_WIDTH, 3 * D_DN), CONV_WIDTH),
        "a_log": jnp.log(jax.random.uniform(ks[8], (N_DN_HEADS,), f32, 1.0, 16.0)),
        "dt_bias": dt + jnp.log(-jnp.expm1(-dt)),
        "dn_norm": gain(ks[10], DN_HEAD_DIM),
        "w_out": nrm(ks[11], (D_MIX, D_MODEL), D_MIX),
        "ffn2_norm": gain(ks[12], D_MODEL),
        "ffn2_w_gate": nrm(ks[13], (D_MODEL, D_FF), D_MODEL),
        "ffn2_w_up": nrm(ks[14], (D_MODEL, D_FF), D_MODEL),
        "ffn2_w_down": nrm(ks[15], (D_FF, D_MODEL), D_FF),
        "final_norm": gain(ks[16], D_MODEL),
    }


def reference(x, ffn1_norm, ffn1_w_gate, ffn1_w_up, ffn1_w_down, mix_norm, w_in, conv_w,
              a_log, dt_bias, dn_norm, w_out, ffn2_norm, ffn2_w_gate, ffn2_w_up, ffn2_w_down,
              final_norm):
    h = x
    for _ in range(DEPTH):
        h = h + 0.5 * _swiglu(_rmsnorm(h, ffn1_norm), ffn1_w_gate, ffn1_w_up, ffn1_w_down)
        h = h + _hybrid_mixer(_rmsnorm(h, mix_norm), w_in, conv_w, a_log, dt_bias, dn_norm, w_out)
        h = h + 0.5 * _swiglu(_rmsnorm(h, ffn2_norm), ffn2_w_gate, ffn2_w_up, ffn2_w_down)
    return _rmsnorm(h, final_norm)
```

```python
import functools
import math

import jax
import jax.numpy as jnp
from jax import lax
from jax.experimental import pallas as pl
from jax.experimental.pallas import tpu as pltpu

EPS = 1e-6
HEAD_DIM = 128
DILATED_CONFIGS = ((128, 1), (512, 4), (2048, 16))
CONV_WIDTH = 4
DN_CHUNK = 128
MASKED_BIAS = -1e30

VMEM_LIMIT_BYTES = 56 * 1024 * 1024
MXU_DTYPE = jnp.bfloat16


def _cparams(*sem):
    return pltpu.CompilerParams(dimension_semantics=sem, vmem_limit_bytes=VMEM_LIMIT_BYTES)


def _dot(a, b):
    return jnp.dot(a.astype(MXU_DTYPE), b.astype(MXU_DTYPE), preferred_element_type=jnp.float32)


def _dot_nt(a, b):
    return lax.dot_general(a.astype(MXU_DTYPE), b.astype(MXU_DTYPE), (((1,), (1,)), ((), ())),
                           preferred_element_type=jnp.float32)


def _silu(x):
    return x * jax.nn.sigmoid(x)


def _rmsnorm_kernel(x_ref, w_ref, o_ref):
    x = x_ref[...]
    y = x * lax.rsqrt(jnp.mean(x * x, axis=-1, keepdims=True) + EPS)
    o_ref[...] = (y * w_ref[...]).astype(o_ref.dtype)


def rmsnorm(x, w, out_dtype, tm=256):
    t, d = x.shape
    return pl.pallas_call(
        _rmsnorm_kernel,
        grid=(t // tm,),
        in_specs=[pl.BlockSpec((tm, d), lambda i: (i, 0)),
                  pl.BlockSpec((1, d), lambda i: (0, 0))],
        out_specs=pl.BlockSpec((tm, d), lambda i: (i, 0)),
        out_shape=jax.ShapeDtypeStruct((t, d), out_dtype),
        compiler_params=_cparams("parallel"),
        name="rmsnorm",
    )(x, w.reshape(1, d))


def _ffn_up_kernel(x_ref, wg_ref, wu_ref, o_ref):
    x = x_ref[...]
    g = _dot(x, wg_ref[...])
    u = _dot(x, wu_ref[...])
    o_ref[...] = (_silu(g) * u).astype(o_ref.dtype)


def ffn_up(xn, wg, wu, bm=1024, bn=512):
    t, d = xn.shape
    f = wg.shape[1]
    return pl.pallas_call(
        _ffn_up_kernel,
        grid=(t // bm, f // bn),
        in_specs=[pl.BlockSpec((bm, d), lambda i, j: (i, 0)),
                  pl.BlockSpec((d, bn), lambda i, j: (0, j)),
                  pl.BlockSpec((d, bn), lambda i, j: (0, j))],
        out_specs=pl.BlockSpec((bm, bn), lambda i, j: (i, j)),
        out_shape=jax.ShapeDtypeStruct((t, f), MXU_DTYPE),
        compiler_params=_cparams("parallel", "parallel"),
        name="ffn_up",
    )(xn, wg, wu)


def _mm_kernel(a_ref, w_ref, o_ref):
    o_ref[...] = _dot(a_ref[...], w_ref[...]).astype(o_ref.dtype)


def matmul(a, w, out_dtype=jnp.float32, bm=1024, bn=1024):
    t, k = a.shape
    n = w.shape[1]
    bn = min(bn, n)
    return pl.pallas_call(
        _mm_kernel,
        grid=(t // bm, n // bn),
        in_specs=[pl.BlockSpec((bm, k), lambda i, j: (i, 0)),
                  pl.BlockSpec((k, bn), lambda i, j: (0, j))],
        out_specs=pl.BlockSpec((bm, bn), lambda i, j: (i, j)),
        out_shape=jax.ShapeDtypeStruct((t, n), out_dtype),
        compiler_params=_cparams("parallel", "parallel"),
        name="matmul",
    )(a, w)


def _mm_res_kernel(a_ref, w_ref, r_ref, o_ref, *, scale):
    part = scale * _dot(a_ref[...], w_ref[...])

    @pl.when(pl.program_id(2) == 0)
    def _():
        o_ref[...] = r_ref[...] + part

    @pl.when(pl.program_id(2) != 0)
    def _():
        o_ref[...] += part


def matmul_residual(a, w, res, scale, bm=1024, bn=1024, nk=4):
    t, k = a.shape
    n = w.shape[1]
    tk = k // nk
    return pl.pallas_call(
        functools.partial(_mm_res_kernel, scale=scale),
        grid=(t // bm, n // bn, nk),
        in_specs=[pl.BlockSpec((bm, tk), lambda i, j, kk: (i, kk)),
                  pl.BlockSpec((tk, bn), lambda i, j, kk: (kk, j)),
                  pl.BlockSpec((bm, bn), lambda i, j, kk: (i, j))],
        out_specs=pl.BlockSpec((bm, bn), lambda i, j, kk: (i, j)),
        out_shape=jax.ShapeDtypeStruct((t, n), jnp.float32),
        compiler_params=_cparams("parallel", "parallel", "arbitrary"),
        name="matmul_residual",
    )(a, w, res)


def _mm2_res_kernel(a1_ref, a2_ref, w1_ref, w2_ref, r_ref, o_ref):
    o_ref[...] = r_ref[...] + (_dot(a1_ref[...], w1_ref[...]) + _dot(a2_ref[...], w2_ref[...]))


def matmul2_residual(a1, a2, w1, w2, res, bm=1024, bn=1024):
    t, k1 = a1.shape
    k2 = a2.shape[1]
    n = w1.shape[1]
    return pl.pallas_call(
        _mm2_res_kernel,
        grid=(t // bm, n // bn),
        in_specs=[pl.BlockSpec((bm, k1), lambda i, j: (i, 0)),
                  pl.BlockSpec((bm, k2), lambda i, j: (i, 0)),
                  pl.BlockSpec((k1, bn), lambda i, j: (0, j)),
                  pl.BlockSpec((k2, bn), lambda i, j: (0, j)),
                  pl.BlockSpec((bm, bn), lambda i, j: (i, j))],
        out_specs=pl.BlockSpec((bm, bn), lambda i, j: (i, j)),
        out_shape=jax.ShapeDtypeStruct((t, n), jnp.float32),
        compiler_params=_cparams("parallel", "parallel"),
        name="matmul2_residual",
    )(a1, a2, w1, w2, res)


def _attn_kernel(q_ref, k_ref, v_ref, o_ref, bias_ref, *, seq, tq):
    @pl.when((pl.program_id(0) == 0) & (pl.program_id(1) == 0))
    def _():
        i = lax.broadcasted_iota(jnp.int32, (tq, seq), 0)
        jj = lax.broadcasted_iota(jnp.int32, (tq, seq), 1)
        dist = i - (jj - (seq - tq))
        count = jnp.zeros((tq, seq), jnp.int32)
        for window, d in DILATED_CONFIGS:
            hit = (dist >= 0) & (dist <= (window // d) * d) & ((dist & (d - 1)) == 0)
            count = count + hit.astype(jnp.int32)
        bias_ref[...] = jnp.where(
            count == 3, math.log(3.0),
            jnp.where(count == 2, math.log(2.0), jnp.where(count == 1, 0.0, MASKED_BIAS)))

    scale = HEAD_DIM ** -0.5
    k = k_ref[...].astype(MXU_DTYPE)
    v = v_ref[...].astype(MXU_DTYPE)
    for qi in range(seq // tq):
        kend = (qi + 1) * tq
        q = q_ref[qi * tq:kend, :]
        s = _dot_nt(q, k[:kend]) * scale + bias_ref[:, seq - kend:]
        m = jnp.max(s, axis=-1, keepdims=True)
        p = jnp.exp(s - m)
        den = jnp.sum(p, axis=-1, keepdims=True)
        o = _dot(p, v[:kend])
        o_ref[qi * tq:kend, :] = (o / den).astype(o_ref.dtype)


def dilated_attention(proj, batch, seq, n_heads, tq=256):
    tq = min(tq, seq)
    for window, d in DILATED_CONFIGS:
        assert d & (d - 1) == 0
    blk = lambda off: pl.BlockSpec((seq, HEAD_DIM), lambda b, h: (b, off + h))
    return pl.pallas_call(
        functools.partial(_attn_kernel, seq=seq, tq=tq),
        grid=(batch, n_heads),
        in_specs=[blk(0), blk(n_heads), blk(2 * n_heads)],
        out_specs=pl.BlockSpec((seq, HEAD_DIM), lambda b, h: (b, h)),
        out_shape=jax.ShapeDtypeStruct((batch * seq, n_heads * HEAD_DIM), MXU_DTYPE),
        scratch_shapes=[pltpu.VMEM((tq, seq), jnp.float32)],
        compiler_params=_cparams("arbitrary", "arbitrary"),
        name="dilated_attention",
    )(proj, proj, proj)


def _shift_rows(x, s, row):
    return jnp.where(row >= s, pltpu.roll(x, s, axis=0), 0.0)


def _causal_conv_silu(x, w, row):
    acc = w[0:1, :] * _shift_rows(x, CONV_WIDTH - 1, row)
    for i in range(1, CONV_WIDTH):
        s = CONV_WIDTH - 1 - i
        acc = acc + w[i:i + 1, :] * (_shift_rows(x, s, row) if s else x)
    return _silu(acc)


def _l2norm(x):
    return x * lax.rsqrt(jnp.sum(x * x, axis=-1, keepdims=True) + EPS)


def _lane_column(x, lane, col):
    picked = jnp.sum(jnp.where(lane == col, x, 0.0), axis=-1, keepdims=True)
    return jnp.broadcast_to(picked, x.shape)


def _unit_lower_inverse(nmat, ci, cj):
    c = nmat.shape[0]
    eye = (ci == cj).astype(jnp.float32)
    m = -jnp.where((ci >> 3) == (cj >> 3), nmat, 0.0)
    tinv = eye + m
    m = _dot(m, m)
    x = _dot(m, jnp.concatenate([m, tinv], axis=1))
    tinv = tinv + x[:, c:]
    tinv = tinv + _dot(x[:, :c], tinv)
    shift = 3
    while (1 << shift) < c:
        off = jnp.where(((ci >> (shift + 1)) == (cj >> (shift + 1))) & ((ci >> shift) != (cj >> shift)),
                        nmat, 0.0)
        tinv = tinv - _dot(tinv, _dot(off, tinv))
        shift += 1
    return tinv


def _deltanet_kernel(q_ref, k_ref, v_ref, z_ref, gate_ref, cq_ref, ck_ref, cv_ref, alog_ref, dtb_ref,
                     gain_ref, o_ref, q_s, k_s, kb_s, kbe_s, vb_s, qd_s, gc_s, w_s, u_s, qkd_s, o_s,
                     *, seq, n_heads):
    c = DN_CHUNK
    h = pl.program_id(1)
    row = lax.broadcasted_iota(jnp.int32, (seq, HEAD_DIM), 0)
    lane = lax.broadcasted_iota(jnp.int32, (seq, HEAD_DIM), 1)

    q = _l2norm(_causal_conv_silu(q_ref[...], cq_ref[...], row)) * (HEAD_DIM ** -0.5)
    k = _l2norm(_causal_conv_silu(k_ref[...], ck_ref[...], row))
    v = _causal_conv_silu(v_ref[...], cv_ref[...], row)
    gates = gate_ref[...]
    beta_all = jax.nn.sigmoid(gates)
    x = gates + dtb_ref[...]
    softplus = jnp.maximum(x, 0.0) + jnp.log1p(jnp.exp(-jnp.abs(x)))
    g_all = -jnp.exp(alog_ref[...]) * softplus
    beta = _lane_column(beta_all, lane, h)
    gc = _lane_column(g_all, lane, n_heads + h)
    s = 1
    while s < c:
        gc = gc + jnp.where((row & (c - 1)) >= s, pltpu.roll(gc, s, axis=0), 0.0)
        s *= 2
    eg = jnp.exp(gc)
    kb = k * beta
    q_s[...] = q
    k_s[...] = k
    kb_s[...] = kb
    kbe_s[...] = kb * eg
    vb_s[...] = v * beta
    qd_s[...] = q * eg
    gc_s[...] = gc

    ci = lax.broadcasted_iota(jnp.int32, (c, c), 0)
    cj = lax.broadcasted_iota(jnp.int32, (c, c), 1)

    def prep(n, carry):
        r = pl.ds(pl.multiple_of(n * c, c), c)
        kn = k_s[r, :]
        gcn = gc_s[r, :]
        decay = jnp.exp(jnp.where(ci >= cj, gcn - gcn.T, -jnp.inf))
        kk = _dot_nt(kb_s[r, :], kn)
        qkd_s[r, :] = _dot_nt(q_s[r, :], kn) * decay
        nmat = jnp.where(ci > cj, kk * decay, 0.0)
        tinv = _unit_lower_inverse(nmat, ci, cj)
        sol = _dot(tinv, jnp.concatenate([kbe_s[r, :], vb_s[r, :]], axis=1))
        w_s[r, :] = sol[:, :HEAD_DIM]
        u_s[r, :] = sol[:, HEAD_DIM:]
        return carry

    lax.fori_loop(0, seq // c, prep, 0)

    def scan(n, state):
        r = pl.ds(pl.multiple_of(n * c, c), c)
        gcn = gc_s[r, :]
        gl = gc_s[pl.ds(n * c + c - 1, 1), :]
        k_dec = k_s[r, :] * jnp.exp(gl - gcn)
        v_new = u_s[r, :] - _dot(w_s[r, :], state)
        o_s[r, :] = _dot(qd_s[r, :], state) + _dot(qkd_s[r, :], v_new)
        return state * jnp.exp(gl) + _dot(k_dec.T, v_new)

    lax.fori_loop(0, seq // c, scan, jnp.zeros((HEAD_DIM, HEAD_DIM), jnp.float32))

    o = o_s[...]
    o = o * lax.rsqrt(jnp.mean(o * o, axis=-1, keepdims=True) + EPS) * gain_ref[...]
    o_ref[...] = (o * _silu(z_ref[...])).astype(o_ref.dtype)


def gated_deltanet(proj, gates, conv_w, a_row, dt_row, gain, batch, seq, n_heads, col0):
    blk = lambda off: pl.BlockSpec((seq, HEAD_DIM), lambda b, h: (b, col0 + off + h))
    cblk = lambda off: pl.BlockSpec((CONV_WIDTH, HEAD_DIM), lambda b, h: (0, off + h))
    row_spec = pl.BlockSpec((1, HEAD_DIM), lambda b, h: (0, 0))
    seq_buf = pltpu.VMEM((seq, HEAD_DIM), jnp.float32)
    return pl.pallas_call(
        functools.partial(_deltanet_kernel, seq=seq, n_heads=n_heads),
        grid=(batch, n_heads),
        in_specs=[blk(0), blk(n_heads), blk(2 * n_heads), blk(3 * n_heads),
                  pl.BlockSpec((seq, HEAD_DIM), lambda b, h: (b, 0)),
                  cblk(0), cblk(n_heads), cblk(2 * n_heads),
                  row_spec, row_spec, row_spec],
        out_specs=pl.BlockSpec((seq, HEAD_DIM), lambda b, h: (b, h)),
        out_shape=jax.ShapeDtypeStruct((batch * seq, n_heads * HEAD_DIM), MXU_DTYPE),
        scratch_shapes=[seq_buf] * 11,
        compiler_params=_cparams("parallel", "parallel"),
        name="gated_deltanet",
    )(proj, proj, proj, proj, gates, conv_w, conv_w, conv_w, a_row, dt_row, gain)


def _ffn(h, norm_w, w_gate, w_up, w_down, f_pad):
    d, f = w_gate.shape
    pad_cols = lambda w: jnp.pad(w.astype(MXU_DTYPE), ((0, 0), (0, f_pad - f)))
    wd = jnp.pad(w_down.astype(MXU_DTYPE), ((0, f_pad - f), (0, 0)))
    act = ffn_up(rmsnorm(h, norm_w, MXU_DTYPE), pad_cols(w_gate), pad_cols(w_up))
    return matmul_residual(act, wd, h, 0.5)


def kernel(x, ffn1_norm, ffn1_w_gate, ffn1_w_up, ffn1_w_down, mix_norm, w_in, conv_w, a_log, dt_bias,
           dn_norm, w_out, ffn2_norm, ffn2_w_gate, ffn2_w_up, ffn2_w_down, final_norm):
    batch, seq, d_model = x.shape
    n_heads = a_log.shape[0]
    d_head_group = n_heads * HEAD_DIM
    f = ffn1_w_gate.shape[1]
    f_pad = -(-f // 2048) * 2048
    h = x.reshape(batch * seq, d_model)

    h = _ffn(h, ffn1_norm, ffn1_w_gate, ffn1_w_up, ffn1_w_down, f_pad)

    n_main = 7 * d_head_group
    xn = rmsnorm(h, mix_norm, MXU_DTYPE)
    proj = matmul(xn, w_in[:, :n_main].astype(MXU_DTYPE))
    w_gates = jnp.pad(w_in[:, n_main:].astype(MXU_DTYPE), ((0, 0), (0, HEAD_DIM - 2 * n_heads)))
    gates = matmul(xn, w_gates)
    lane_pad = lambda t: jnp.pad(t.astype(jnp.float32), (n_heads, HEAD_DIM - 2 * n_heads)).reshape(1, HEAD_DIM)
    attn = dilated_attention(proj, batch, seq, n_heads)
    dn = gated_deltanet(proj, gates, conv_w, lane_pad(a_log), lane_pad(dt_bias),
                        dn_norm.astype(jnp.float32).reshape(1, HEAD_DIM), batch, seq, n_heads,
                        col0=3 * n_heads)
    wo = w_out.astype(MXU_DTYPE)
    h = matmul2_residual(attn, dn, wo[:d_head_group], wo[d_head_group:], h)

    h = _ffn(h, ffn2_norm, ffn2_w_gate, ffn2_w_up, ffn2_w_down, f_pad)
    return rmsnorm(h, final_norm, jnp.float32).reshape(batch, seq, d_model)
```

```python
import functools
import math

import jax
import jax.numpy as jnp
from jax import lax
from jax.experimental import pallas as pl
from jax.experimental.pallas import tpu as pltpu

EPS = 1e-6
HEAD_DIM = 128
DILATED_CONFIGS = ((128, 1), (512, 4), (2048, 16))
CONV_WIDTH = 4
DN_CHUNK = 128
DN_HEADS_PER_STEP = 2
DN_CHUNKS_PER_ITER = 4
MASKED_BIAS = -1e30
SUBLANES = 8

VMEM_LIMIT_BYTES = 56 * 1024 * 1024
MXU_DTYPE = jnp.bfloat16


def _cparams(*sem):
    return pltpu.CompilerParams(dimension_semantics=sem, vmem_limit_bytes=VMEM_LIMIT_BYTES)


def _dot(a, b):
    return jnp.dot(a.astype(MXU_DTYPE), b.astype(MXU_DTYPE), preferred_element_type=jnp.float32)


def _dot_nt(a, b):
    return lax.dot_general(a.astype(MXU_DTYPE), b.astype(MXU_DTYPE), (((1,), (1,)), ((), ())),
                           preferred_element_type=jnp.float32)


def _silu(x):
    hx = 0.5 * x
    return hx + hx * jnp.tanh(hx)


def _rmsnorm_kernel(x_ref, w_ref, o_ref):
    x = x_ref[...]
    y = x * lax.rsqrt(jnp.mean(x * x, axis=-1, keepdims=True) + EPS)
    o_ref[...] = (y * w_ref[...]).astype(o_ref.dtype)


def rmsnorm(x, w, out_dtype, tm=256):
    t, d = x.shape
    return pl.pallas_call(
        _rmsnorm_kernel,
        grid=(t // tm,),
        in_specs=[pl.BlockSpec((tm, d), lambda i: (i, 0)),
                  pl.BlockSpec((1, d), lambda i: (0, 0))],
        out_specs=pl.BlockSpec((tm, d), lambda i: (i, 0)),
        out_shape=jax.ShapeDtypeStruct((t, d), out_dtype),
        compiler_params=_cparams("parallel"),
        name="rmsnorm",
    )(x, w.reshape(1, d))


def _ffn_up_kernel(x_ref, wg_ref, wu_ref, o_ref):
    x = x_ref[...]
    g = _dot(x, wg_ref[...])
    u = _dot(x, wu_ref[...])
    o_ref[...] = (_silu(g) * u).astype(o_ref.dtype)


def ffn_up(xn, wg, wu, bm=1024, bn=256):
    t, d = xn.shape
    f = wg.shape[1]
    return pl.pallas_call(
        _ffn_up_kernel,
        grid=(t // bm, f // bn),
        in_specs=[pl.BlockSpec((bm, d), lambda i, j: (i, 0)),
                  pl.BlockSpec((d, bn), lambda i, j: (0, j)),
                  pl.BlockSpec((d, bn), lambda i, j: (0, j))],
        out_specs=pl.BlockSpec((bm, bn), lambda i, j: (i, j)),
        out_shape=jax.ShapeDtypeStruct((t, f), MXU_DTYPE),
        compiler_params=_cparams("parallel", "parallel"),
        name="ffn_up",
    )(xn, wg, wu)


def _mm_kernel(a_ref, w_ref, o_ref, *, act):
    acc = _dot(a_ref[...], w_ref[...])
    o_ref[...] = (act(acc) if act else acc).astype(o_ref.dtype)


def matmul(a, w, col0, n, out_dtype, act=None, bm=1024, bn=512):
    t, k = a.shape
    assert col0 % bn == 0 and n % bn == 0
    return pl.pallas_call(
        functools.partial(_mm_kernel, act=act),
        grid=(t // bm, n // bn),
        in_specs=[pl.BlockSpec((bm, k), lambda i, j: (i, 0)),
                  pl.BlockSpec((k, bn), lambda i, j: (0, col0 // bn + j))],
        out_specs=pl.BlockSpec((bm, bn), lambda i, j: (i, j)),
        out_shape=jax.ShapeDtypeStruct((t, n), out_dtype),
        compiler_params=_cparams("parallel", "parallel"),
        name="matmul",
    )(a, w)


def _shifted_rows(x, s):
    row = lax.broadcasted_iota(jnp.int32, (SUBLANES, x.shape[1]), 0)
    rolled = pltpu.roll(x, s, axis=0)
    head = jnp.where(row >= s, rolled[:SUBLANES], 0.0)
    return jnp.concatenate([head, rolled[SUBLANES:]], axis=0)


def _l2norm_heads(x, scale):
    heads = []
    for g in range(x.shape[1] // HEAD_DIM):
        xg = x[:, g * HEAD_DIM:(g + 1) * HEAD_DIM]
        heads.append(xg * (lax.rsqrt(jnp.sum(xg * xg, axis=-1, keepdims=True) + EPS) * scale))
    return jnp.concatenate(heads, axis=1)


def _proj_conv_kernel(a_ref, w_ref, cw_ref, o_ref, *, q_blocks, k_blocks):
    acc = _dot(a_ref[...], w_ref[...])
    cw = cw_ref[...]
    y = cw[0:1, :] * _shifted_rows(acc, CONV_WIDTH - 1)
    for i in range(1, CONV_WIDTH):
        s = CONV_WIDTH - 1 - i
        y = y + cw[i:i + 1, :] * (_shifted_rows(acc, s) if s else acc)
    y = _silu(y)
    j = pl.program_id(1)

    @pl.when(j < q_blocks)
    def _():
        o_ref[...] = _l2norm_heads(y, HEAD_DIM ** -0.5)

    @pl.when((j >= q_blocks) & (j < q_blocks + k_blocks))
    def _():
        o_ref[...] = _l2norm_heads(y, 1.0)

    @pl.when(j >= q_blocks + k_blocks)
    def _():
        o_ref[...] = y


def proj_conv_qkv(a, w, conv_w, col0, seq, bn=256):
    t, k = a.shape
    n = conv_w.shape[1]
    width = n // 3
    assert col0 % bn == 0 and width % bn == 0 and bn % HEAD_DIM == 0
    return pl.pallas_call(
        functools.partial(_proj_conv_kernel, q_blocks=width // bn, k_blocks=width // bn),
        grid=(t // seq, n // bn),
        in_specs=[pl.BlockSpec((seq, k), lambda i, j: (i, 0)),
                  pl.BlockSpec((k, bn), lambda i, j: (0, col0 // bn + j)),
                  pl.BlockSpec((CONV_WIDTH, bn), lambda i, j: (0, j))],
        out_specs=pl.BlockSpec((seq, bn), lambda i, j: (i, j)),
        out_shape=jax.ShapeDtypeStruct((t, n), jnp.float32),
        compiler_params=_cparams("parallel", "parallel"),
        name="proj_conv_qkv",
    )(a, w, conv_w)


def _mm_res_kernel(a_ref, w_ref, r_ref, o_ref, *, scale):
    o_ref[...] = r_ref[...] + scale * _dot(a_ref[...], w_ref[...])


def matmul_residual(a, w, res, scale, bm=512, bn=512):
    t, k = a.shape
    n = w.shape[1]
    return pl.pallas_call(
        functools.partial(_mm_res_kernel, scale=scale),
        grid=(t // bm, n // bn),
        in_specs=[pl.BlockSpec((bm, k), lambda i, j: (i, 0)),
                  pl.BlockSpec((k, bn), lambda i, j: (0, j)),
                  pl.BlockSpec((bm, bn), lambda i, j: (i, j))],
        out_specs=pl.BlockSpec((bm, bn), lambda i, j: (i, j)),
        out_shape=jax.ShapeDtypeStruct((t, n), jnp.float32),
        compiler_params=_cparams("parallel", "parallel"),
        name="matmul_residual",
    )(a, w, res)


def _mm2_res_kernel(a1_ref, a2_ref, w1_ref, w2_ref, r_ref, o_ref):
    o_ref[...] = r_ref[...] + (_dot(a1_ref[...], w1_ref[...]) + _dot(a2_ref[...], w2_ref[...]))


def matmul2_residual(a1, a2, w, res, bm=1024, bn=512):
    t, k1 = a1.shape
    assert a2.shape[1] == k1 and w.shape[0] == 2 * k1
    n = w.shape[1]
    return pl.pallas_call(
        _mm2_res_kernel,
        grid=(t // bm, n // bn),
        in_specs=[pl.BlockSpec((bm, k1), lambda i, j: (i, 0)),
                  pl.BlockSpec((bm, k1), lambda i, j: (i, 0)),
                  pl.BlockSpec((k1, bn), lambda i, j: (0, j)),
                  pl.BlockSpec((k1, bn), lambda i, j: (1, j)),
                  pl.BlockSpec((bm, bn), lambda i, j: (i, j))],
        out_specs=pl.BlockSpec((bm, bn), lambda i, j: (i, j)),
        out_shape=jax.ShapeDtypeStruct((t, n), jnp.float32),
        compiler_params=_cparams("parallel", "parallel"),
        name="matmul2_residual",
    )(a1, a2, w, w, res)


def _attn_kernel(q_ref, k_ref, v_ref, o_ref, bias_ref, *, seq, tq):
    @pl.when((pl.program_id(0) == 0) & (pl.program_id(1) == 0))
    def _():
        i = lax.broadcasted_iota(jnp.int32, (tq, seq), 0)
        jj = lax.broadcasted_iota(jnp.int32, (tq, seq), 1)
        dist = i - (jj - (seq - tq))
        count = jnp.zeros((tq, seq), jnp.int32)
        for window, d in DILATED_CONFIGS:
            hit = (dist >= 0) & (dist <= (window // d) * d) & ((dist & (d - 1)) == 0)
            count = count + hit.astype(jnp.int32)
        bias_ref[...] = jnp.where(
            count == 3, math.log(3.0),
            jnp.where(count == 2, math.log(2.0), jnp.where(count == 1, 0.0, MASKED_BIAS)))

    scale = HEAD_DIM ** -0.5
    k = k_ref[...].astype(MXU_DTYPE)
    v = v_ref[...].astype(MXU_DTYPE)
    for qi in range(seq // tq):
        kend = (qi + 1) * tq
        q = q_ref[qi * tq:kend, :]
        s = _dot_nt(q, k[:kend]) * scale + bias_ref[:, seq - kend:]
        m = jnp.max(s, axis=-1, keepdims=True)
        p = jnp.exp(s - m)
        den = jnp.sum(p, axis=-1, keepdims=True)
        o = _dot(p, v[:kend])
        o_ref[qi * tq:kend, :] = (o / den).astype(o_ref.dtype)


def dilated_attention(proj, batch, seq, n_heads, tq=256):
    tq = min(tq, seq)
    for window, d in DILATED_CONFIGS:
        assert d & (d - 1) == 0
    blk = lambda off: pl.BlockSpec((seq, HEAD_DIM), lambda b, h: (b, off + h))
    return pl.pallas_call(
        functools.partial(_attn_kernel, seq=seq, tq=tq),
        grid=(batch, n_heads),
        in_specs=[blk(0), blk(n_heads), blk(2 * n_heads)],
        out_specs=pl.BlockSpec((seq, HEAD_DIM), lambda b, h: (b, h)),
        out_shape=jax.ShapeDtypeStruct((batch * seq, n_heads * HEAD_DIM), MXU_DTYPE),
        scratch_shapes=[pltpu.VMEM((tq, seq), jnp.float32)],
        compiler_params=_cparams("arbitrary", "arbitrary"),
        name="dilated_attention",
    )(proj, proj, proj)


def _dn_gates_kernel(x_ref, wb_ref, wa_ref, alog_ref, dtb_ref, beta_ref, gc_ref, eg_ref, kd_ref):
    c = DN_CHUNK
    x = x_ref[...]
    beta_ref[...] = jax.nn.sigmoid(_dot(x, wb_ref[...]))
    z = _dot(x, wa_ref[...]) + dtb_ref[...]
    softplus = jnp.maximum(z, 0.0) + jnp.log1p(jnp.exp(-jnp.abs(z)))
    gc = -jnp.exp(alog_ref[...]) * softplus
    row = lax.broadcasted_iota(jnp.int32, gc.shape, 0)
    s = 1
    while s < c:
        gc = gc + jnp.where((row & (c - 1)) >= s, pltpu.roll(gc, s, axis=0), 0.0)
        s *= 2
    gc_ref[...] = gc
    eg_ref[...] = jnp.exp(gc)
    for n in range(gc.shape[0] // c):
        last = gc[n * c + c - 1:n * c + c, :]
        kd_ref[n * c:(n + 1) * c, :] = jnp.exp(last - gc[n * c:(n + 1) * c, :])


def dn_gates(xn, w_beta, w_decay, a_row, dt_row, bm=1024):
    t, d = xn.shape
    out = jax.ShapeDtypeStruct((t, HEAD_DIM), jnp.float32)
    tok = pl.BlockSpec((bm, HEAD_DIM), lambda i: (i, 0))
    wspec = pl.BlockSpec((d, HEAD_DIM), lambda i: (0, 0))
    row_spec = pl.BlockSpec((1, HEAD_DIM), lambda i: (0, 0))
    return pl.pallas_call(
        _dn_gates_kernel,
        grid=(t // bm,),
        in_specs=[pl.BlockSpec((bm, d), lambda i: (i, 0)), wspec, wspec, row_spec, row_spec],
        out_specs=[tok, tok, tok, tok],
        out_shape=[out, out, out, out],
        compiler_params=_cparams("parallel"),
        name="dn_gates",
    )(xn, w_beta, w_decay, a_row, dt_row)


def _lane_column(x, lane, col):
    picked = jnp.sum(jnp.where(lane == col, x, 0.0), axis=-1, keepdims=True)
    return jnp.broadcast_to(picked, x.shape)


def _unit_lower_inverses(nmats, ci, cj):
    c = nmats[0].shape[0]
    eye = (ci == cj).astype(jnp.float32)
    same8 = (ci >> 3) == (cj >> 3)
    ms = [-jnp.where(same8, nmat, 0.0) for nmat in nmats]
    tinvs = [eye + m for m in ms]
    ms = [_dot(m, m) for m in ms]
    xs = [_dot(m, jnp.concatenate([m, tinv], axis=1)) for m, tinv in zip(ms, tinvs)]
    tinvs = [tinv + x[:, c:] for tinv, x in zip(tinvs, xs)]
    ys = [_dot(x[:, :c], tinv) for x, tinv in zip(xs, tinvs)]
    tinvs = [tinv + y for tinv, y in zip(tinvs, ys)]
    shift = 3
    while (1 << shift) < c:
        pick = ((ci >> (shift + 1)) == (cj >> (shift + 1))) & ((ci >> shift) != (cj >> shift))
        ys = [_dot(jnp.where(pick, nmat, 0.0), tinv) for nmat, tinv in zip(nmats, tinvs)]
        ys = [_dot(tinv, y) for tinv, y in zip(tinvs, ys)]
        tinvs = [tinv - y for tinv, y in zip(tinvs, ys)]
        shift += 1
    return tinvs


def _deltanet_kernel(q_ref, k_ref, v_ref, z_ref, beta_ref, gc_ref, eg_ref, kd_ref, gain_ref, o_ref, q_s, k_s, kb_s, rhs_s, qd_s, kdt_s, gc_s, egl_s, w_s, u_s, qkd_s, o_s,
                     *, seq):
    c = DN_CHUNK
    hd = HEAD_DIM
    n_chunks = seq // c
    heads = DN_HEADS_PER_STEP
    lane = lax.broadcasted_iota(jnp.int32, (seq, hd), 1)

    for e in range(heads):
        cols = slice(e * hd, (e + 1) * hd)
        head = pl.program_id(1) * heads + e
        beta = _lane_column(beta_ref[...], lane, head)
        gc = _lane_column(gc_ref[...], lane, head)
        eg = _lane_column(eg_ref[...], lane, head)
        kd = _lane_column(kd_ref[...], lane, head)
        q = q_ref[:, cols]
        k = k_ref[:, cols]
        v = v_ref[:, cols]
        kb = k * beta
        kdt = (k * kd).T
        q_s[e] = q.astype(q_s.dtype).reshape(n_chunks, c, hd)
        k_s[e] = k.astype(k_s.dtype).reshape(n_chunks, c, hd)
        kb_s[e] = kb.astype(kb_s.dtype).reshape(n_chunks, c, hd)
        rhs_s[e] = jnp.concatenate([kb * eg, v * beta], axis=1).astype(rhs_s.dtype).reshape(n_chunks, c, 2 * hd)
        qd_s[e] = (q * eg).astype(qd_s.dtype).reshape(n_chunks, c, hd)
        gc_s[e] = gc.reshape(n_chunks, c, hd)
        for n in range(n_chunks):
            kdt_s[e, n] = kdt[:, n * c:(n + 1) * c].astype(kdt_s.dtype)
            egl_s[e, n] = eg[n * c + c - SUBLANES:(n + 1) * c, :]

    ci = lax.broadcasted_iota(jnp.int32, (c, c), 0)
    cj = lax.broadcasted_iota(jnp.int32, (c, c), 1)

    def prep(it, carry):
        chains = [(e, it * DN_CHUNKS_PER_ITER + g) for g in range(DN_CHUNKS_PER_ITER) for e in range(heads)]
        ks = [k_s[e, n] for e, n in chains]
        kbs = [kb_s[e, n] for e, n in chains]
        qs = [q_s[e, n] for e, n in chains]
        gcs = [gc_s[e, n] for e, n in chains]
        rhss = [rhs_s[e, n] for e, n in chains]
        kks = [_dot_nt(kb, k) for kb, k in zip(kbs, ks)]
        qks = [_dot_nt(q, k) for q, k in zip(qs, ks)]
        decays = [jnp.exp(jnp.where(ci >= cj, gc - gc.T, -jnp.inf)) for gc in gcs]
        nmats = [jnp.where(ci > cj, kk * decay, 0.0) for kk, decay in zip(kks, decays)]
        tinvs = _unit_lower_inverses(nmats, ci, cj)
        sols = [_dot(tinv, rhs) for tinv, rhs in zip(tinvs, rhss)]
        for (e, n), qk, decay, sol in zip(chains, qks, decays, sols):
            qkd_s[e, n] = (qk * decay).astype(qkd_s.dtype)
            w_s[e, n] = sol[:, :hd].astype(w_s.dtype)
            u_s[e, n] = sol[:, hd:]
        return carry

    lax.fori_loop(0, n_chunks // DN_CHUNKS_PER_ITER, prep, 0)

    def scan(n, states):
        r = range(heads)
        us = [u_s[e, n] for e in r]
        ws = [w_s[e, n] for e in r]
        qds = [qd_s[e, n] for e in r]
        qkds = [qkd_s[e, n] for e in r]
        kdts = [kdt_s[e, n] for e in r]
        egls = [egl_s[e, n][SUBLANES - 1:SUBLANES, :] for e in r]
        v_news = [us[e] - _dot(ws[e], states[e]) for e in r]
        outs = [_dot(qds[e], states[e]) + _dot(qkds[e], v_news[e]) for e in r]
        new_states = tuple(states[e] * egls[e] + _dot(kdts[e], v_news[e]) for e in r)
        for e in r:
            o_s[e, n] = outs[e]
        return new_states

    lax.fori_loop(0, n_chunks, scan, tuple(jnp.zeros((hd, hd), jnp.float32) for _ in range(heads)))

    for e in range(heads):
        cols = slice(e * hd, (e + 1) * hd)
        o = o_s[e].reshape(seq, hd)
        o = o * lax.rsqrt(jnp.mean(o * o, axis=-1, keepdims=True) + EPS) * gain_ref[...]
        o_ref[:, cols] = (o * z_ref[:, cols]).astype(o_ref.dtype)


def gated_deltanet(qkv, zact, gates, gain, batch, seq, n_heads):
    heads = DN_HEADS_PER_STEP
    c = DN_CHUNK
    n_chunks = seq // c
    assert n_heads % heads == 0 and n_chunks % DN_CHUNKS_PER_ITER == 0
    wide = heads * HEAD_DIM
    blk = lambda off: pl.BlockSpec((seq, wide), lambda b, h: (b, off // heads + h))
    gate_spec = pl.BlockSpec((seq, HEAD_DIM), lambda b, h: (b, 0))
    per_chunk = lambda width, dtype: pltpu.VMEM((heads, n_chunks, c, width), dtype)
    return pl.pallas_call(
        functools.partial(_deltanet_kernel, seq=seq),
        grid=(batch, n_heads // heads),
        in_specs=[blk(0), blk(n_heads), blk(2 * n_heads), blk(0),
                  gate_spec, gate_spec, gate_spec, gate_spec,
                  pl.BlockSpec((1, HEAD_DIM), lambda b, h: (0, 0))],
        out_specs=pl.BlockSpec((seq, wide), lambda b, h: (b, h)),
        out_shape=jax.ShapeDtypeStruct((batch * seq, n_heads * HEAD_DIM), MXU_DTYPE),
        scratch_shapes=[
            per_chunk(HEAD_DIM, MXU_DTYPE),
            per_chunk(HEAD_DIM, MXU_DTYPE),
            per_chunk(HEAD_DIM, MXU_DTYPE),
            per_chunk(2 * HEAD_DIM, MXU_DTYPE),
            per_chunk(HEAD_DIM, MXU_DTYPE),
            per_chunk(c, MXU_DTYPE),
            per_chunk(HEAD_DIM, jnp.float32),
            pltpu.VMEM((heads, n_chunks, SUBLANES, HEAD_DIM), jnp.float32),
            per_chunk(HEAD_DIM, MXU_DTYPE),
            per_chunk(HEAD_DIM, jnp.float32),
            per_chunk(c, MXU_DTYPE),
            per_chunk(HEAD_DIM, jnp.float32),
        ],
        compiler_params=_cparams("parallel", "parallel"),
        name="gated_deltanet",
    )(qkv, qkv, qkv, zact, *gates, gain)


def _ffn(h, norm_w, w_gate, w_up, w_down):
    act = ffn_up(rmsnorm(h, norm_w, MXU_DTYPE), w_gate, w_up)
    return matmul_residual(act, w_down.astype(MXU_DTYPE), h, 0.5)


def kernel(x, ffn1_norm, ffn1_w_gate, ffn1_w_up, ffn1_w_down, mix_norm, w_in, conv_w, a_log, dt_bias,
           dn_norm, w_out, ffn2_norm, ffn2_w_gate, ffn2_w_up, ffn2_w_down, final_norm):
    batch, seq, d_model = x.shape
    n_heads = a_log.shape[0]
    d_head_group = n_heads * HEAD_DIM
    h = x.reshape(batch * seq, d_model)

    h = _ffn(h, ffn1_norm, ffn1_w_gate, ffn1_w_up, ffn1_w_down)

    xn = rmsnorm(h, mix_norm, MXU_DTYPE)
    qkv_attn = matmul(xn, w_in, 0, 3 * d_head_group, MXU_DTYPE)
    qkv_dn = proj_conv_qkv(xn, w_in, conv_w, 3 * d_head_group, seq)
    zact = matmul(xn, w_in, 6 * d_head_group, d_head_group, jnp.float32, act=_silu)
    n_main = 7 * d_head_group
    lane_pad = lambda t: jnp.pad(t.astype(jnp.float32), ((0, 0), (0, HEAD_DIM - n_heads)))
    gates = dn_gates(xn, lane_pad(w_in[:, n_main:n_main + n_heads]), lane_pad(w_in[:, n_main + n_heads:]),
                     lane_pad(a_log.reshape(1, n_heads)), lane_pad(dt_bias.reshape(1, n_heads)))
    attn = dilated_attention(qkv_attn, batch, seq, n_heads)
    dn = gated_deltanet(qkv_dn, zact, gates, dn_norm.astype(jnp.float32).reshape(1, HEAD_DIM),
                        batch, seq, n_heads)
    h = matmul2_residual(attn, dn, w_out, h)

    h = _ffn(h, ffn2_norm, ffn2_w_gate, ffn2_w_up, ffn2_w_down)
    return rmsnorm(h, final_norm, jnp.float32).reshape(batch, seq, d_model)
```

```python
import functools
import math

import jax
import jax.numpy as jnp
from jax import lax
from jax.experimental import pallas as pl
from jax.experimental.pallas import tpu as pltpu

EPS = 1e-6
HEAD_DIM = 128
DILATED_CONFIGS = ((128, 1), (512, 4), (2048, 16))
CONV_WIDTH = 4
DN_CHUNK = 128
DN_HEADS_PER_STEP = 2
DN_CHUNKS_PER_ITER = 4
MASKED_BIAS = -1e30
SUBLANES = 8

VMEM_LIMIT_BYTES = 56 * 1024 * 1024
MXU_DTYPE = jnp.bfloat16


def _cparams(*sem):
    return pltpu.CompilerParams(dimension_semantics=sem, vmem_limit_bytes=VMEM_LIMIT_BYTES)


def _dot(a, b):
    return jnp.dot(a.astype(MXU_DTYPE), b.astype(MXU_DTYPE), preferred_element_type=jnp.float32)


def _dot_nt(a, b):
    return lax.dot_general(a.astype(MXU_DTYPE), b.astype(MXU_DTYPE), (((1,), (1,)), ((), ())),
                           preferred_element_type=jnp.float32)


def _silu(x):
    hx = 0.5 * x
    return hx + hx * jnp.tanh(hx)


def _rmsnorm_kernel(x_ref, w_ref, o_ref):
    x = x_ref[...]
    y = x * lax.rsqrt(jnp.mean(x * x, axis=-1, keepdims=True) + EPS)
    o_ref[...] = (y * w_ref[...]).astype(o_ref.dtype)


def rmsnorm(x, w, out_dtype, tm=256):
    t, d = x.shape
    return pl.pallas_call(
        _rmsnorm_kernel,
        grid=(t // tm,),
        in_specs=[pl.BlockSpec((tm, d), lambda i: (i, 0)),
                  pl.BlockSpec((1, d), lambda i: (0, 0))],
        out_specs=pl.BlockSpec((tm, d), lambda i: (i, 0)),
        out_shape=jax.ShapeDtypeStruct((t, d), out_dtype),
        compiler_params=_cparams("parallel"),
        name="rmsnorm",
    )(x, w.reshape(1, d))


def _ffn_up_kernel(x_ref, wg_ref, wu_ref, o_ref):
    x = x_ref[...]
    g = _dot(x, wg_ref[...])
    u = _dot(x, wu_ref[...])
    o_ref[...] = (_silu(g) * u).astype(o_ref.dtype)


def ffn_up(xn, wg, wu, bm=1024, bn=256):
    t, d = xn.shape
    f = wg.shape[1]
    return pl.pallas_call(
        _ffn_up_kernel,
        grid=(t // bm, f // bn),
        in_specs=[pl.BlockSpec((bm, d), lambda i, j: (i, 0)),
                  pl.BlockSpec((d, bn), lambda i, j: (0, j)),
                  pl.BlockSpec((d, bn), lambda i, j: (0, j))],
        out_specs=pl.BlockSpec((bm, bn), lambda i, j: (i, j)),
        out_shape=jax.ShapeDtypeStruct((t, f), MXU_DTYPE),
        compiler_params=_cparams("parallel", "parallel"),
        name="ffn_up",
    )(xn, wg, wu)


def _mm_kernel(a_ref, w_ref, o_ref, *, act):
    acc = _dot_nt(a_ref[...], w_ref[...])
    o_ref[...] = (act(acc) if act else acc).astype(o_ref.dtype)


def matmul_nt(a, wt, row0, n, out_dtype, act=None, bm=1024, bn=512):
    t, k = a.shape
    assert row0 % bn == 0 and n % bn == 0
    return pl.pallas_call(
        functools.partial(_mm_kernel, act=act),
        grid=(t // bm, n // bn),
        in_specs=[pl.BlockSpec((bm, k), lambda i, j: (i, 0)),
                  pl.BlockSpec((bn, k), lambda i, j: (row0 // bn + j, 0))],
        out_specs=pl.BlockSpec((bm, bn), lambda i, j: (i, j)),
        out_shape=jax.ShapeDtypeStruct((t, n), out_dtype),
        compiler_params=_cparams("parallel", "parallel"),
        name="matmul_nt",
    )(a, wt)


def _shifted_rows(x, s):
    row = lax.broadcasted_iota(jnp.int32, (SUBLANES, x.shape[1]), 0)
    rolled = pltpu.roll(x, s, axis=0)
    head = jnp.where(row >= s, rolled[:SUBLANES], 0.0)
    return jnp.concatenate([head, rolled[SUBLANES:]], axis=0)


def _l2norm_heads(x, scale):
    heads = []
    for g in range(x.shape[1] // HEAD_DIM):
        xg = x[:, g * HEAD_DIM:(g + 1) * HEAD_DIM]
        heads.append(xg * (lax.rsqrt(jnp.sum(xg * xg, axis=-1, keepdims=True) + EPS) * scale))
    return jnp.concatenate(heads, axis=1)


def _proj_conv_kernel(a_ref, w_ref, cw_ref, o_ref, *, q_blocks, k_blocks):
    acc = _dot_nt(a_ref[...], w_ref[...])
    cw = cw_ref[...]
    y = cw[0:1, :] * _shifted_rows(acc, CONV_WIDTH - 1)
    for i in range(1, CONV_WIDTH):
        s = CONV_WIDTH - 1 - i
        y = y + cw[i:i + 1, :] * (_shifted_rows(acc, s) if s else acc)
    y = _silu(y)
    j = pl.program_id(1)

    @pl.when(j < q_blocks)
    def _():
        o_ref[...] = _l2norm_heads(y, HEAD_DIM ** -0.5)

    @pl.when((j >= q_blocks) & (j < q_blocks + k_blocks))
    def _():
        o_ref[...] = _l2norm_heads(y, 1.0)

    @pl.when(j >= q_blocks + k_blocks)
    def _():
        o_ref[...] = y


def proj_conv_qkv(a, wt, conv_w, row0, seq, bn=256):
    t, k = a.shape
    n = conv_w.shape[1]
    width = n // 3
    assert row0 % bn == 0 and width % bn == 0 and bn % HEAD_DIM == 0
    return pl.pallas_call(
        functools.partial(_proj_conv_kernel, q_blocks=width // bn, k_blocks=width // bn),
        grid=(t // seq, n // bn),
        in_specs=[pl.BlockSpec((seq, k), lambda i, j: (i, 0)),
                  pl.BlockSpec((bn, k), lambda i, j: (row0 // bn + j, 0)),
                  pl.BlockSpec((CONV_WIDTH, bn), lambda i, j: (0, j))],
        out_specs=pl.BlockSpec((seq, bn), lambda i, j: (i, j)),
        out_shape=jax.ShapeDtypeStruct((t, n), jnp.float32),
        compiler_params=_cparams("parallel", "parallel"),
        name="proj_conv_qkv",
    )(a, wt, conv_w)


def _mm_res_kernel(a_ref, w_ref, r_ref, o_ref, *, scale):
    o_ref[...] = r_ref[...] + scale * _dot(a_ref[...], w_ref[...])


def matmul_residual(a, w, res, scale, bm=512, bn=512):
    t, k = a.shape
    n = w.shape[1]
    return pl.pallas_call(
        functools.partial(_mm_res_kernel, scale=scale),
        grid=(t // bm, n // bn),
        in_specs=[pl.BlockSpec((bm, k), lambda i, j: (i, 0)),
                  pl.BlockSpec((k, bn), lambda i, j: (0, j)),
                  pl.BlockSpec((bm, bn), lambda i, j: (i, j))],
        out_specs=pl.BlockSpec((bm, bn), lambda i, j: (i, j)),
        out_shape=jax.ShapeDtypeStruct((t, n), jnp.float32),
        compiler_params=_cparams("parallel", "parallel"),
        name="matmul_residual",
    )(a, w, res)


def _mm2_res_kernel(a1_ref, a2_ref, w1_ref, w2_ref, r_ref, o_ref):
    o_ref[...] = r_ref[...] + (_dot(a1_ref[...], w1_ref[...]) + _dot(a2_ref[...], w2_ref[...]))


def matmul2_residual(a1, a2, w, res, bm=1024, bn=512):
    t, k1 = a1.shape
    assert a2.shape[1] == k1 and w.shape[0] == 2 * k1
    n = w.shape[1]
    return pl.pallas_call(
        _mm2_res_kernel,
        grid=(t // bm, n // bn),
        in_specs=[pl.BlockSpec((bm, k1), lambda i, j: (i, 0)),
                  pl.BlockSpec((bm, k1), lambda i, j: (i, 0)),
                  pl.BlockSpec((k1, bn), lambda i, j: (0, j)),
                  pl.BlockSpec((k1, bn), lambda i, j: (1, j)),
                  pl.BlockSpec((bm, bn), lambda i, j: (i, j))],
        out_specs=pl.BlockSpec((bm, bn), lambda i, j: (i, j)),
        out_shape=jax.ShapeDtypeStruct((t, n), jnp.float32),
        compiler_params=_cparams("parallel", "parallel"),
        name="matmul2_residual",
    )(a1, a2, w, w, res)


def _attn_kernel(q_ref, k_ref, v_ref, o_ref, bias_ref, *, seq, tq):
    @pl.when((pl.program_id(0) == 0) & (pl.program_id(1) == 0))
    def _():
        i = lax.broadcasted_iota(jnp.int32, (tq, seq), 0)
        jj = lax.broadcasted_iota(jnp.int32, (tq, seq), 1)
        dist = i - (jj - (seq - tq))
        count = jnp.zeros((tq, seq), jnp.int32)
        for window, d in DILATED_CONFIGS:
            hit = (dist >= 0) & (dist <= (window // d) * d) & ((dist & (d - 1)) == 0)
            count = count + hit.astype(jnp.int32)
        bias_ref[...] = jnp.where(
            count == 3, math.log2(3.0),
            jnp.where(count == 2, 1.0, jnp.where(count == 1, 0.0, MASKED_BIAS)))

    scale = HEAD_DIM ** -0.5 * math.log2(math.e)
    k = k_ref[...].astype(MXU_DTYPE)
    v = v_ref[...].astype(MXU_DTYPE)
    for qi in range(seq // tq):
        kend = (qi + 1) * tq
        q = q_ref[qi * tq:kend, :]
        s = _dot_nt(q, k[:kend]) * scale + bias_ref[:, seq - kend:]
        m = jnp.max(s, axis=-1, keepdims=True)
        p = jnp.exp2(s - m)
        den = jnp.sum(p, axis=-1, keepdims=True)
        o = _dot(p, v[:kend])
        o_ref[qi * tq:kend, :] = (o / den).astype(o_ref.dtype)


def dilated_attention(proj, batch, seq, n_heads, tq=256):
    tq = min(tq, seq)
    for window, d in DILATED_CONFIGS:
        assert d & (d - 1) == 0
    blk = lambda off: pl.BlockSpec((seq, HEAD_DIM), lambda b, h: (b, off + h))
    return pl.pallas_call(
        functools.partial(_attn_kernel, seq=seq, tq=tq),
        grid=(batch, n_heads),
        in_specs=[blk(0), blk(n_heads), blk(2 * n_heads)],
        out_specs=pl.BlockSpec((seq, HEAD_DIM), lambda b, h: (b, h)),
        out_shape=jax.ShapeDtypeStruct((batch * seq, n_heads * HEAD_DIM), MXU_DTYPE),
        scratch_shapes=[pltpu.VMEM((tq, seq), jnp.float32)],
        compiler_params=_cparams("arbitrary", "arbitrary"),
        name="dilated_attention",
    )(proj, proj, proj)


def _dn_gates_kernel(x_ref, wb_ref, wa_ref, alog_ref, dtb_ref, beta_ref, gc_ref):
    c = DN_CHUNK
    x = x_ref[...]
    beta_ref[...] = jax.nn.sigmoid(_dot_nt(x, wb_ref[...]))
    z = _dot_nt(x, wa_ref[...]) + dtb_ref[...]
    softplus = jnp.maximum(z, 0.0) + jnp.log1p(jnp.exp(-jnp.abs(z)))
    gc = -jnp.exp(alog_ref[...]) * softplus
    row = lax.broadcasted_iota(jnp.int32, gc.shape, 0)
    s = 1
    while s < c:
        gc = gc + jnp.where((row & (c - 1)) >= s, pltpu.roll(gc, s, axis=0), 0.0)
        s *= 2
    gc_ref[...] = gc


def dn_gates(xn, wt_beta, wt_decay, a_row, dt_row, bm=1024):
    t, d = xn.shape
    out = jax.ShapeDtypeStruct((t, HEAD_DIM), jnp.float32)
    tok = pl.BlockSpec((bm, HEAD_DIM), lambda i: (i, 0))
    wspec = pl.BlockSpec((HEAD_DIM, d), lambda i: (0, 0))
    row_spec = pl.BlockSpec((1, HEAD_DIM), lambda i: (0, 0))
    return pl.pallas_call(
        _dn_gates_kernel,
        grid=(t // bm,),
        in_specs=[pl.BlockSpec((bm, d), lambda i: (i, 0)), wspec, wspec, row_spec, row_spec],
        out_specs=[tok, tok],
        out_shape=[out, out],
        compiler_params=_cparams("parallel"),
        name="dn_gates",
    )(xn, wt_beta, wt_decay, a_row, dt_row)


def _lane_column(x, lane, col):
    picked = jnp.sum(jnp.where(lane == col, x, 0.0), axis=-1, keepdims=True)
    return jnp.broadcast_to(picked, x.shape)


def _unit_lower_inverses(nmats, ci, cj):
    c = nmats[0].shape[0]
    eye = (ci == cj).astype(jnp.float32)
    same8 = (ci >> 3) == (cj >> 3)
    ms = [-jnp.where(same8, nmat, 0.0) for nmat in nmats]
    tinvs = [eye + m for m in ms]
    ms = [_dot(m, m) for m in ms]
    yield
    xs = [_dot(m, jnp.concatenate([m, tinv], axis=1)) for m, tinv in zip(ms, tinvs)]
    yield
    tinvs = [tinv + x[:, c:] for tinv, x in zip(tinvs, xs)]
    ys = [_dot(x[:, :c], tinv) for x, tinv in zip(xs, tinvs)]
    yield
    tinvs = [tinv + y for tinv, y in zip(tinvs, ys)]
    shift = 3
    while (1 << shift) < c:
        pick = ((ci >> (shift + 1)) == (cj >> (shift + 1))) & ((ci >> shift) != (cj >> shift))
        ys = [_dot(jnp.where(pick, nmat, 0.0), tinv) for nmat, tinv in zip(nmats, tinvs)]
        yield
        ys = [_dot(tinv, y) for tinv, y in zip(tinvs, ys)]
        yield
        tinvs = [tinv - y for tinv, y in zip(tinvs, ys)]
        shift += 1
    return tinvs


def _interleave(first, second):
    results = [None, None]
    live = [first, second]
    while any(g is not None for g in live):
        for idx, g in enumerate(live):
            if g is None:
                continue
            try:
                next(g)
            except StopIteration as stop:
                results[idx] = stop.value
                live[idx] = None
    return results


def _deltanet_kernel(q_ref, k_ref, v_ref, z_ref, beta_ref, gc_ref, gain_ref, o_ref,
                     q_s, k_s, kb_s, rhs_s, qd_s, kdt_s, gc_s, egl_s, w_s, u_s, qkd_s, o_s, *, seq):
    c = DN_CHUNK
    hd = HEAD_DIM
    n_chunks = seq // c
    group = DN_CHUNKS_PER_ITER
    heads = DN_HEADS_PER_STEP
    lane = lax.broadcasted_iota(jnp.int32, (seq, hd), 1)

    for e in range(heads):
        cols = slice(e * hd, (e + 1) * hd)
        head = pl.program_id(1) * heads + e
        beta = _lane_column(beta_ref[...], lane, head)
        gc = _lane_column(gc_ref[...], lane, head).reshape(n_chunks, c, hd)
        eg = jnp.exp(gc)
        kd = jnp.exp(gc[:, c - 1:c, :] - gc)
        q = q_ref[:, cols]
        k = k_ref[:, cols]
        kb = k * beta
        kdt = (k * kd.reshape(seq, hd)).T
        eg2 = eg.reshape(seq, hd)
        q_s[e] = q.astype(q_s.dtype).reshape(n_chunks, c, hd)
        k_s[e] = k.astype(k_s.dtype).reshape(n_chunks, c, hd)
        kb_s[e] = kb.astype(kb_s.dtype).reshape(n_chunks, c, hd)
        rhs_s[e] = jnp.concatenate([kb * eg2, v_ref[:, cols] * beta], axis=1
                                   ).astype(rhs_s.dtype).reshape(n_chunks, c, 2 * hd)
        qd_s[e] = (q * eg2).astype(qd_s.dtype).reshape(n_chunks, c, hd)
        gc_s[e] = gc
        egl_s[e] = eg[:, c - SUBLANES:, :]
        for n in range(n_chunks):
            kdt_s[e, n] = kdt[:, n * c:(n + 1) * c].astype(kdt_s.dtype)

    ci = lax.broadcasted_iota(jnp.int32, (c, c), 0)
    cj = lax.broadcasted_iota(jnp.int32, (c, c), 1)

    def prep_stages(it):
        chains = [(e, it * group + g) for g in range(group) for e in range(heads)]
        ks = [k_s[e, n] for e, n in chains]
        kbs = [kb_s[e, n] for e, n in chains]
        qs = [q_s[e, n] for e, n in chains]
        gcs = [gc_s[e, n] for e, n in chains]
        rhss = [rhs_s[e, n] for e, n in chains]
        yield
        kks = [_dot_nt(kb, k) for kb, k in zip(kbs, ks)]
        qks = [_dot_nt(q, k) for q, k in zip(qs, ks)]
        yield
        decays = [jnp.exp(jnp.where(ci >= cj, gc - gc.T, -jnp.inf)) for gc in gcs]
        nmats = [jnp.where(ci > cj, kk * decay, 0.0) for kk, decay in zip(kks, decays)]
        tinvs = yield from _unit_lower_inverses(nmats, ci, cj)
        sols = [_dot(tinv, rhs) for tinv, rhs in zip(tinvs, rhss)]
        yield
        return [(e, n, (qk * decay).astype(qkd_s.dtype), sol[:, :hd].astype(w_s.dtype), sol[:, hd:])
                for (e, n), qk, decay, sol in zip(chains, qks, decays, sols)]

    def store_prepped(prepped):
        for e, n, qkd, w, u in prepped:
            qkd_s[e, n] = qkd
            w_s[e, n] = w
            u_s[e, n] = u

    def scan_stages(it, states):
        r = range(heads)
        loaded = []
        for g in range(group):
            n = it * group + g
            loaded.append(([u_s[e, n] for e in r], [w_s[e, n] for e in r], [qd_s[e, n] for e in r],
                           [qkd_s[e, n] for e in r], [kdt_s[e, n] for e in r],
                           [egl_s[e, n][SUBLANES - 1:SUBLANES, :] for e in r]))
        yield
        outs = []
        for us, ws, qds, qkds, kdts, egls in loaded:
            v_news = [us[e] - _dot(ws[e], states[e]) for e in r]
            from_state = [_dot(qds[e], states[e]) for e in r]
            yield
            outs.append([from_state[e] + _dot(qkds[e], v_news[e]) for e in r])
            states = tuple(states[e] * egls[e] + _dot(kdts[e], v_news[e]) for e in r)
            yield
        return states, outs

    def store_outs(it, outs):
        for g, chunk_outs in enumerate(outs):
            for e in range(heads):
                o_s[e, it * group + g] = chunk_outs[e]

    def run(gen):
        return _interleave(gen, None)[0]

    n_groups = n_chunks // group
    store_prepped(run(prep_stages(0)))

    def body(it, states):
        prepped, (states, outs) = _interleave(prep_stages(it), scan_stages(it - 1, states))
        store_prepped(prepped)
        store_outs(it - 1, outs)
        return states

    states = lax.fori_loop(1, n_groups, body, tuple(jnp.zeros((hd, hd), jnp.float32) for _ in range(heads)))
    _, outs = run(scan_stages(n_groups - 1, states))
    store_outs(n_groups - 1, outs)

    for e in range(heads):
        cols = slice(e * hd, (e + 1) * hd)
        o = o_s[e].reshape(seq, hd)
        o = o * lax.rsqrt(jnp.mean(o * o, axis=-1, keepdims=True) + EPS) * gain_ref[...]
        o_ref[:, cols] = (o * z_ref[:, cols]).astype(o_ref.dtype)


def gated_deltanet(qkv, zact, gates, gain, batch, seq, n_heads):
    heads = DN_HEADS_PER_STEP
    c = DN_CHUNK
    n_chunks = seq // c
    assert n_heads % heads == 0 and n_chunks % DN_CHUNKS_PER_ITER == 0
    wide = heads * HEAD_DIM
    blk = lambda off: pl.BlockSpec((seq, wide), lambda b, h: (b, off // heads + h))
    gate_spec = pl.BlockSpec((seq, HEAD_DIM), lambda b, h: (b, 0))
    per_chunk = lambda width, dtype: pltpu.VMEM((heads, n_chunks, c, width), dtype)
    return pl.pallas_call(
        functools.partial(_deltanet_kernel, seq=seq),
        grid=(batch, n_heads // heads),
        in_specs=[blk(0), blk(n_heads), blk(2 * n_heads), blk(0),
                  gate_spec, gate_spec,
                  pl.BlockSpec((1, HEAD_DIM), lambda b, h: (0, 0))],
        out_specs=pl.BlockSpec((seq, wide), lambda b, h: (b, h)),
        out_shape=jax.ShapeDtypeStruct((batch * seq, n_heads * HEAD_DIM), MXU_DTYPE),
        scratch_shapes=[
            per_chunk(HEAD_DIM, MXU_DTYPE),
            per_chunk(HEAD_DIM, MXU_DTYPE),
            per_chunk(HEAD_DIM, MXU_DTYPE),
            per_chunk(2 * HEAD_DIM, MXU_DTYPE),
            per_chunk(HEAD_DIM, MXU_DTYPE),
            per_chunk(c, MXU_DTYPE),
            per_chunk(HEAD_DIM, jnp.float32),
            pltpu.VMEM((heads, n_chunks, SUBLANES, HEAD_DIM), jnp.float32),
            per_chunk(HEAD_DIM, MXU_DTYPE),
            per_chunk(HEAD_DIM, jnp.float32),
            per_chunk(c, MXU_DTYPE),
            per_chunk(HEAD_DIM, jnp.float32),
        ],
        compiler_params=_cparams("parallel", "parallel"),
        name="gated_deltanet",
    )(qkv, qkv, qkv, zact, *gates, gain)


def _ffn(h, norm_w, w_gate, w_up, w_down):
    act = ffn_up(rmsnorm(h, norm_w, MXU_DTYPE), w_gate, w_up)
    return matmul_residual(act, w_down.astype(MXU_DTYPE), h, 0.5)


def kernel(x, ffn1_norm, ffn1_w_gate, ffn1_w_up, ffn1_w_down, mix_norm, w_in, conv_w, a_log, dt_bias,
           dn_norm, w_out, ffn2_norm, ffn2_w_gate, ffn2_w_up, ffn2_w_down, final_norm):
    batch, seq, d_model = x.shape
    n_heads = a_log.shape[0]
    d_head_group = n_heads * HEAD_DIM
    h = x.reshape(batch * seq, d_model)

    h = _ffn(h, ffn1_norm, ffn1_w_gate, ffn1_w_up, ffn1_w_down)

    wt = w_in.T
    xn = rmsnorm(h, mix_norm, MXU_DTYPE)
    qkv_attn = matmul_nt(xn, wt, 0, 3 * d_head_group, MXU_DTYPE)
    qkv_dn = proj_conv_qkv(xn, wt, conv_w, 3 * d_head_group, seq)
    zact = matmul_nt(xn, wt, 6 * d_head_group, d_head_group, jnp.float32, act=_silu)
    n_main = 7 * d_head_group
    rows_pad = lambda t: jnp.pad(t.astype(jnp.float32), ((0, HEAD_DIM - n_heads), (0, 0)))
    lane_pad = lambda t: jnp.pad(t.astype(jnp.float32).reshape(1, n_heads), ((0, 0), (0, HEAD_DIM - n_heads)))
    gates = dn_gates(xn, rows_pad(wt[n_main:n_main + n_heads]), rows_pad(wt[n_main + n_heads:]),
                     lane_pad(a_log), lane_pad(dt_bias))
    attn = dilated_attention(qkv_attn, batch, seq, n_heads)
    dn = gated_deltanet(qkv_dn, zact, gates, dn_norm.astype(jnp.float32).reshape(1, HEAD_DIM),
                        batch, seq, n_heads)
    h = matmul2_residual(attn, dn, w_out, h)

    h = _ffn(h, ffn2_norm, ffn2_w_gate, ffn2_w_up, ffn2_w_down)
    return rmsnorm(h, final_norm, jnp.float32).reshape(batch, seq, d_model)
```

```python
import functools
import math

import jax
import jax.numpy as jnp
from jax import lax
from jax.experimental import pallas as pl
from jax.experimental.pallas import tpu as pltpu

EPS = 1e-6
HEAD_DIM = 128
DILATED_CONFIGS = ((128, 1), (512, 4), (2048, 16))
CONV_WIDTH = 4
DN_CHUNK = 128
DN_HEADS_PER_STEP = 2
DN_CHUNKS_PER_ITER = 4
MASKED_BIAS = -1e30
SUBLANES = 8

VMEM_LIMIT_BYTES = 56 * 1024 * 1024
FFN_VMEM_LIMIT_BYTES = 60 * 1024 * 1024
MXU_DTYPE = jnp.bfloat16


def _cparams(*sem):
    return pltpu.CompilerParams(dimension_semantics=sem, vmem_limit_bytes=VMEM_LIMIT_BYTES)


def _dot(a, b):
    return jnp.dot(a.astype(MXU_DTYPE), b.astype(MXU_DTYPE), preferred_element_type=jnp.float32)


def _dot_nt(a, b):
    return lax.dot_general(a.astype(MXU_DTYPE), b.astype(MXU_DTYPE), (((1,), (1,)), ((), ())),
                           preferred_element_type=jnp.float32)


def _silu(x):
    hx = 0.5 * x
    return hx + hx * jnp.tanh(hx)


PROJ_CONV_ROW_CHUNK = 512
FFN_ROW_CHUNK = 128
FFN_K_CHUNK = 1024
FFN_N_CHUNK = 512


def _rmsnorm_rows(src_ref, w_ref, dst_ref):
    def body(r, carry):
        rows = pl.ds(pl.multiple_of(r * FFN_ROW_CHUNK, FFN_ROW_CHUNK), FFN_ROW_CHUNK)
        x = src_ref[rows, :]
        y = x * lax.rsqrt(jnp.mean(x * x, axis=-1, keepdims=True) + EPS)
        dst_ref[rows, :] = (y * w_ref[...]).astype(dst_ref.dtype)
        return carry
    lax.fori_loop(0, src_ref.shape[0] // FFN_ROW_CHUNK, body, 0)


def _ffn_kernel(h_hbm, nw_ref, wg_ref, wu_ref, wd_ref, onw_ref, *rest, emit_res):
    if emit_res:
        res_hbm, norm_hbm, acc, xn_s, sem = rest
    else:
        norm_hbm, acc, xn_s, sem = rest
    i, j = pl.program_id(0), pl.program_id(1)
    n_i, n_j = pl.num_programs(0), pl.num_programs(1)
    bm, d = acc.shape
    tile = lambda ref, t: ref.at[pl.ds(pl.multiple_of(t * bm, bm), bm), :]
    in_copy = lambda t: pltpu.make_async_copy(tile(h_hbm, t), acc, sem.at[0])
    res_copy = lambda t: pltpu.make_async_copy(acc, tile(res_hbm, t), sem.at[1])
    norm_copy = lambda t: pltpu.make_async_copy(xn_s if emit_res else acc, tile(norm_hbm, t), sem.at[2])

    def wait_results(t):
        if emit_res:
            res_copy(t).wait()
        norm_copy(t).wait()

    @pl.when(j == 0)
    def _():
        @pl.when(i > 0)
        def _():
            wait_results(i - 1)
        in_copy(i).start()
        in_copy(i).wait()
        _rmsnorm_rows(acc, nw_ref, xn_s)

    g = u = None
    for kc in range(d // FFN_K_CHUNK):
        ks = slice(kc * FFN_K_CHUNK, (kc + 1) * FFN_K_CHUNK)
        xk = xn_s[:, ks]
        gk, uk = _dot(xk, wg_ref[ks, :]), _dot(xk, wu_ref[ks, :])
        g, u = (gk, uk) if g is None else (g + gk, u + uk)
    act = ((0.5 * _silu(g)) * u).astype(MXU_DTYPE)
    for nc in range(d // FFN_N_CHUNK):
        ns = slice(nc * FFN_N_CHUNK, (nc + 1) * FFN_N_CHUNK)
        acc[:, ns] += _dot(act, wd_ref[:, ns])

    @pl.when(j == n_j - 1)
    def _():
        if emit_res:
            res_copy(i).start()
            _rmsnorm_rows(acc, onw_ref, xn_s)
        else:
            _rmsnorm_rows(acc, onw_ref, acc)
        norm_copy(i).start()

        @pl.when(i == n_i - 1)
        def _():
            wait_results(i)


def ffn(h, norm_w, w_gate, w_up, w_down, out_norm_w, emit_res, norm_dtype, bm=1024, bf=256):
    t, d = h.shape
    f = w_gate.shape[1]
    assert t % bm == 0 and f % bf == 0 and d % FFN_K_CHUNK == 0 and d % FFN_N_CHUNK == 0
    assert emit_res or norm_dtype == jnp.float32
    any_spec = pl.BlockSpec(memory_space=pl.ANY)
    row_spec = pl.BlockSpec((1, d), lambda i, j: (0, 0))
    norm_shape = jax.ShapeDtypeStruct((t, d), norm_dtype)
    return pl.pallas_call(
        functools.partial(_ffn_kernel, emit_res=emit_res),
        grid=(t // bm, f // bf),
        in_specs=[any_spec, row_spec,
                  pl.BlockSpec((d, bf), lambda i, j: (0, j)),
                  pl.BlockSpec((d, bf), lambda i, j: (0, j)),
                  pl.BlockSpec((bf, d), lambda i, j: (j, 0)),
                  row_spec],
        out_specs=[any_spec, any_spec] if emit_res else any_spec,
        out_shape=[jax.ShapeDtypeStruct((t, d), jnp.float32), norm_shape] if emit_res else norm_shape,
        scratch_shapes=[pltpu.VMEM((bm, d), jnp.float32), pltpu.VMEM((bm, d), MXU_DTYPE),
                        pltpu.SemaphoreType.DMA((3,))],
        compiler_params=pltpu.CompilerParams(dimension_semantics=("arbitrary", "arbitrary"),
                                             vmem_limit_bytes=FFN_VMEM_LIMIT_BYTES),
        name="ffn",
    )(h, norm_w.reshape(1, d), w_gate, w_up, w_down, out_norm_w.reshape(1, d))


def _mm_kernel(a_ref, w_ref, o_ref, *, act):
    acc = _dot_nt(a_ref[...], w_ref[...])
    o_ref[...] = (act(acc) if act else acc).astype(o_ref.dtype)


def matmul_nt(a, wt, row0, n, out_dtype, act=None, bm=1024, bn=512):
    t, k = a.shape
    assert row0 % bn == 0 and n % bn == 0
    return pl.pallas_call(
        functools.partial(_mm_kernel, act=act),
        grid=(t // bm, n // bn),
        in_specs=[pl.BlockSpec((bm, k), lambda i, j: (i, 0)),
                  pl.BlockSpec((bn, k), lambda i, j: (row0 // bn + j, 0))],
        out_specs=pl.BlockSpec((bm, bn), lambda i, j: (i, j)),
        out_shape=jax.ShapeDtypeStruct((t, n), out_dtype),
        compiler_params=_cparams("parallel", "parallel"),
        name="matmul_nt",
    )(a, wt)


def _proj_conv_kernel(a_ref, w_ref, cw_ref, o_ref, acc_s, *, q_blocks, k_blocks):
    seq, bn = o_ref.shape
    chunk = min(PROJ_CONV_ROW_CHUNK, seq)
    w = w_ref[...].astype(MXU_DTYPE)
    cw = cw_ref[...]
    j = pl.program_id(1)
    is_v = j >= q_blocks + k_blocks
    qk_scale = jnp.where(j < q_blocks, HEAD_DIM ** -0.5, 1.0)
    row = lax.broadcasted_iota(jnp.int32, (SUBLANES, bn), 0)
    acc_s[:SUBLANES, :] = jnp.zeros((SUBLANES, bn), jnp.float32)
    for r in range(seq // chunk):
        acc_s[SUBLANES + r * chunk:SUBLANES + (r + 1) * chunk, :] = _dot_nt(a_ref[r * chunk:(r + 1) * chunk, :], w)
    for r in range(seq // chunk):
        y = None
        for i in range(CONV_WIDTH):
            start = SUBLANES + r * chunk - (CONV_WIDTH - 1 - i)
            term = cw[i:i + 1, :] * acc_s[start:start + chunk, :]
            y = term if y is None else y + term
        y = _silu(y)
        heads = []
        for g in range(bn // HEAD_DIM):
            yg = y[:, g * HEAD_DIM:(g + 1) * HEAD_DIM]
            inv = lax.rsqrt(jnp.sum(yg * yg, axis=-1, keepdims=True) + EPS) * qk_scale
            heads.append(yg * jnp.where(is_v, 1.0, inv))
        o_ref[r * chunk:(r + 1) * chunk, :] = jnp.concatenate(heads, axis=1)


def proj_conv_qkv(a, wt, conv_w, row0, seq, bn=256):
    t, k = a.shape
    n = conv_w.shape[1]
    width = n // 3
    assert row0 % bn == 0 and width % bn == 0 and bn % HEAD_DIM == 0
    return pl.pallas_call(
        functools.partial(_proj_conv_kernel, q_blocks=width // bn, k_blocks=width // bn),
        grid=(t // seq, n // bn),
        in_specs=[pl.BlockSpec((seq, k), lambda i, j: (i, 0)),
                  pl.BlockSpec((bn, k), lambda i, j: (row0 // bn + j, 0)),
                  pl.BlockSpec((CONV_WIDTH, bn), lambda i, j: (0, j))],
        out_specs=pl.BlockSpec((seq, bn), lambda i, j: (i, j)),
        out_shape=jax.ShapeDtypeStruct((t, n), jnp.float32),
        scratch_shapes=[pltpu.VMEM((SUBLANES + seq, bn), jnp.float32)],
        compiler_params=_cparams("parallel", "parallel"),
        name="proj_conv_qkv",
    )(a, wt, conv_w)


def _mm2_res_kernel(a1_ref, a2_ref, w1_ref, w2_ref, r_ref, o_ref):
    o_ref[...] = r_ref[...] + (_dot(a1_ref[...], w1_ref[...]) + _dot(a2_ref[...], w2_ref[...]))


def matmul2_residual(a1, a2, w, res, bm=1024, bn=512):
    t, k1 = a1.shape
    assert a2.shape[1] == k1 and w.shape[0] == 2 * k1
    n = w.shape[1]
    return pl.pallas_call(
        _mm2_res_kernel,
        grid=(t // bm, n // bn),
        in_specs=[pl.BlockSpec((bm, k1), lambda i, j: (i, 0)),
                  pl.BlockSpec((bm, k1), lambda i, j: (i, 0)),
                  pl.BlockSpec((k1, bn), lambda i, j: (0, j)),
                  pl.BlockSpec((k1, bn), lambda i, j: (1, j)),
                  pl.BlockSpec((bm, bn), lambda i, j: (i, j))],
        out_specs=pl.BlockSpec((bm, bn), lambda i, j: (i, j)),
        out_shape=jax.ShapeDtypeStruct((t, n), jnp.float32),
        compiler_params=_cparams("parallel", "parallel"),
        name="matmul2_residual",
    )(a1, a2, w, w, res)


def _attn_kernel(q_ref, k_ref, v_ref, o_ref, bias_ref, *, seq, tq):
    @pl.when((pl.program_id(0) == 0) & (pl.program_id(1) == 0))
    def _():
        i = lax.broadcasted_iota(jnp.int32, (tq, seq), 0)
        jj = lax.broadcasted_iota(jnp.int32, (tq, seq), 1)
        dist = i - (jj - (seq - tq))
        count = jnp.zeros((tq, seq), jnp.int32)
        for window, d in DILATED_CONFIGS:
            hit = (dist >= 0) & (dist <= (window // d) * d) & ((dist & (d - 1)) == 0)
            count = count + hit.astype(jnp.int32)
        bias_ref[...] = jnp.where(
            count == 3, math.log2(3.0),
            jnp.where(count == 2, 1.0, jnp.where(count == 1, 0.0, MASKED_BIAS)))

    scale = HEAD_DIM ** -0.5 * math.log2(math.e)
    k = k_ref[...].astype(MXU_DTYPE)
    v = v_ref[...].astype(MXU_DTYPE)
    for qi in range(seq // tq):
        kend = (qi + 1) * tq
        q = q_ref[qi * tq:kend, :]
        s = _dot_nt(q, k[:kend]) * scale + bias_ref[:, seq - kend:]
        m = jnp.max(s, axis=-1, keepdims=True)
        p = jnp.exp2(s - m)
        den = jnp.sum(p, axis=-1, keepdims=True)
        o = _dot(p, v[:kend])
        o_ref[qi * tq:kend, :] = (o / den).astype(o_ref.dtype)


def dilated_attention(proj, batch, seq, n_heads, tq=256):
    tq = min(tq, seq)
    for window, d in DILATED_CONFIGS:
        assert d & (d - 1) == 0
    blk = lambda off: pl.BlockSpec((seq, HEAD_DIM), lambda b, h: (b, off + h))
    return pl.pallas_call(
        functools.partial(_attn_kernel, seq=seq, tq=tq),
        grid=(batch, n_heads),
        in_specs=[blk(0), blk(n_heads), blk(2 * n_heads)],
        out_specs=pl.BlockSpec((seq, HEAD_DIM), lambda b, h: (b, h)),
        out_shape=jax.ShapeDtypeStruct((batch * seq, n_heads * HEAD_DIM), MXU_DTYPE),
        scratch_shapes=[pltpu.VMEM((tq, seq), jnp.float32)],
        compiler_params=_cparams("arbitrary", "arbitrary"),
        name="dilated_attention",
    )(proj, proj, proj)


def _dn_gates_kernel(x_ref, wb_ref, wa_ref, alog_ref, dtb_ref, beta_ref, gc_ref):
    c = DN_CHUNK
    x = x_ref[...]
    beta_ref[...] = jax.nn.sigmoid(_dot_nt(x, wb_ref[...]))
    z = _dot_nt(x, wa_ref[...]) + dtb_ref[...]
    softplus = jnp.maximum(z, 0.0) + jnp.log1p(jnp.exp(-jnp.abs(z)))
    gc = -jnp.exp(alog_ref[...]) * softplus
    row = lax.broadcasted_iota(jnp.int32, gc.shape, 0)
    s = 1
    while s < c:
        gc = gc + jnp.where((row & (c - 1)) >= s, pltpu.roll(gc, s, axis=0), 0.0)
        s *= 2
    gc_ref[...] = gc


def dn_gates(xn, wt_beta, wt_decay, a_row, dt_row, bm=1024):
    t, d = xn.shape
    out = jax.ShapeDtypeStruct((t, HEAD_DIM), jnp.float32)
    tok = pl.BlockSpec((bm, HEAD_DIM), lambda i: (i, 0))
    wspec = pl.BlockSpec((HEAD_DIM, d), lambda i: (0, 0))
    row_spec = pl.BlockSpec((1, HEAD_DIM), lambda i: (0, 0))
    return pl.pallas_call(
        _dn_gates_kernel,
        grid=(t // bm,),
        in_specs=[pl.BlockSpec((bm, d), lambda i: (i, 0)), wspec, wspec, row_spec, row_spec],
        out_specs=[tok, tok],
        out_shape=[out, out],
        compiler_params=_cparams("parallel"),
        name="dn_gates",
    )(xn, wt_beta, wt_decay, a_row, dt_row)


def _lane_column(x, lane, col):
    picked = jnp.sum(jnp.where(lane == col, x, 0.0), axis=-1, keepdims=True)
    return jnp.broadcast_to(picked, x.shape)


def _unit_lower_inverses(nmats, ci, cj):
    c = nmats[0].shape[0]
    eye = (ci == cj).astype(jnp.float32)
    same8 = (ci >> 3) == (cj >> 3)
    ms = [-jnp.where(same8, nmat, 0.0) for nmat in nmats]
    tinvs = [eye + m for m in ms]
    ms = [_dot(m, m) for m in ms]
    yield
    xs = [_dot(m, jnp.concatenate([m, tinv], axis=1)) for m, tinv in zip(ms, tinvs)]
    yield
    tinvs = [tinv + x[:, c:] for tinv, x in zip(tinvs, xs)]
    ys = [_dot(x[:, :c], tinv) for x, tinv in zip(xs, tinvs)]
    yield
    tinvs = [tinv + y for tinv, y in zip(tinvs, ys)]
    shift = 3
    while (1 << shift) < c:
        pick = ((ci >> (shift + 1)) == (cj >> (shift + 1))) & ((ci >> shift) != (cj >> shift))
        ys = [_dot(jnp.where(pick, nmat, 0.0), tinv) for nmat, tinv in zip(nmats, tinvs)]
        yield
        ys = [_dot(tinv, y) for tinv, y in zip(tinvs, ys)]
        yield
        tinvs = [tinv - y for tinv, y in zip(tinvs, ys)]
        shift += 1
    return tinvs


def _interleave(first, second):
    results = [None, None]
    live = [first, second]
    while any(g is not None for g in live):
        for idx, g in enumerate(live):
            if g is None:
                continue
            try:
                next(g)
            except StopIteration as stop:
                results[idx] = stop.value
                live[idx] = None
    return results


def _deltanet_kernel(q_ref, k_ref, v_ref, z_ref, beta_ref, gc_ref, gain_ref, o_ref,
                     q_s, k_s, kb_s, rhs_s, qd_s, kdt_s, gc_s, egl_s, w_s, u_s, qkd_s, o_s, *, seq):
    c = DN_CHUNK
    hd = HEAD_DIM
    n_chunks = seq // c
    group = DN_CHUNKS_PER_ITER
    heads = DN_HEADS_PER_STEP
    lane = lax.broadcasted_iota(jnp.int32, (seq, hd), 1)

    for e in range(heads):
        cols = slice(e * hd, (e + 1) * hd)
        head = pl.program_id(1) * heads + e
        beta = _lane_column(beta_ref[...], lane, head)
        gc = _lane_column(gc_ref[...], lane, head).reshape(n_chunks, c, hd)
        eg = jnp.exp(gc)
        kd = jnp.exp(gc[:, c - 1:c, :] - gc)
        q = q_ref[:, cols]
        k = k_ref[:, cols]
        kb = k * beta
        kdt = (k * kd.reshape(seq, hd)).T
        eg2 = eg.reshape(seq, hd)
        q_s[e] = q.astype(q_s.dtype).reshape(n_chunks, c, hd)
        k_s[e] = k.astype(k_s.dtype).reshape(n_chunks, c, hd)
        kb_s[e] = kb.astype(kb_s.dtype).reshape(n_chunks, c, hd)
        rhs_s[e] = jnp.concatenate([kb * eg2, v_ref[:, cols] * beta], axis=1
                                   ).astype(rhs_s.dtype).reshape(n_chunks, c, 2 * hd)
        qd_s[e] = (q * eg2).astype(qd_s.dtype).reshape(n_chunks, c, hd)
        gc_s[e] = gc
        egl_s[e] = eg[:, c - SUBLANES:, :]
        for n in range(n_chunks):
            kdt_s[e, n] = kdt[:, n * c:(n + 1) * c].astype(kdt_s.dtype)

    ci = lax.broadcasted_iota(jnp.int32, (c, c), 0)
    cj = lax.broadcasted_iota(jnp.int32, (c, c), 1)

    def prep_stages(it):
        chains = [(e, it * group + g) for g in range(group) for e in range(heads)]
        ks = [k_s[e, n] for e, n in chains]
        kbs = [kb_s[e, n] for e, n in chains]
        qs = [q_s[e, n] for e, n in chains]
        gcs = [gc_s[e, n] for e, n in chains]
        rhss = [rhs_s[e, n] for e, n in chains]
        yield
        kks = [_dot_nt(kb, k) for kb, k in zip(kbs, ks)]
        qks = [_dot_nt(q, k) for q, k in zip(qs, ks)]
        yield
        decays = [jnp.exp(jnp.where(ci >= cj, gc - gc.T, -jnp.inf)) for gc in gcs]
        nmats = [jnp.where(ci > cj, kk * decay, 0.0) for kk, decay in zip(kks, decays)]
        tinvs = yield from _unit_lower_inverses(nmats, ci, cj)
        sols = [_dot(tinv, rhs) for tinv, rhs in zip(tinvs, rhss)]
        yield
        return [(e, n, (qk * decay).astype(qkd_s.dtype), sol[:, :hd].astype(w_s.dtype), sol[:, hd:])
                for (e, n), qk, decay, sol in zip(chains, qks, decays, sols)]

    def store_prepped(prepped):
        for e, n, qkd, w, u in prepped:
            qkd_s[e, n] = qkd
            w_s[e, n] = w
            u_s[e, n] = u

    def scan_stages(it, states):
        r = range(heads)
        loaded = []
        for g in range(group):
            n = it * group + g
            loaded.append(([u_s[e, n] for e in r], [w_s[e, n] for e in r], [qd_s[e, n] for e in r],
                           [qkd_s[e, n] for e in r], [kdt_s[e, n] for e in r],
                           [egl_s[e, n][SUBLANES - 1:SUBLANES, :] for e in r]))
        yield
        outs = []
        for us, ws, qds, qkds, kdts, egls in loaded:
            v_news = [us[e] - _dot(ws[e], states[e]) for e in r]
            from_state = [_dot(qds[e], states[e]) for e in r]
            yield
            outs.append([from_state[e] + _dot(qkds[e], v_news[e]) for e in r])
            states = tuple(states[e] * egls[e] + _dot(kdts[e], v_news[e]) for e in r)
            yield
        return states, outs

    def store_outs(it, outs):
        for g, chunk_outs in enumerate(outs):
            for e in range(heads):
                o_s[e, it * group + g] = chunk_outs[e]

    def run(gen):
        return _interleave(gen, None)[0]

    n_groups = n_chunks // group
    store_prepped(run(prep_stages(0)))

    def body(it, states):
        prepped, (states, outs) = _interleave(prep_stages(it), scan_stages(it - 1, states))
        store_prepped(prepped)
        store_outs(it - 1, outs)
        return states

    states = lax.fori_loop(1, n_groups, body, tuple(jnp.zeros((hd, hd), jnp.float32) for _ in range(heads)))
    _, outs = run(scan_stages(n_groups - 1, states))
    store_outs(n_groups - 1, outs)

    for e in range(heads):
        cols = slice(e * hd, (e + 1) * hd)
        o = o_s[e].reshape(seq, hd)
        o = o * lax.rsqrt(jnp.mean(o * o, axis=-1, keepdims=True) + EPS) * gain_ref[...]
        o_ref[:, cols] = (o * z_ref[:, cols]).astype(o_ref.dtype)


def gated_deltanet(qkv, zact, gates, gain, batch, seq, n_heads):
    heads = DN_HEADS_PER_STEP
    c = DN_CHUNK
    n_chunks = seq // c
    assert n_heads % heads == 0 and n_chunks % DN_CHUNKS_PER_ITER == 0
    wide = heads * HEAD_DIM
    blk = lambda off: pl.BlockSpec((seq, wide), lambda b, h: (b, off // heads + h))
    gate_spec = pl.BlockSpec((seq, HEAD_DIM), lambda b, h: (b, 0))
    per_chunk = lambda width, dtype: pltpu.VMEM((heads, n_chunks, c, width), dtype)
    return pl.pallas_call(
        functools.partial(_deltanet_kernel, seq=seq),
        grid=(batch, n_heads // heads),
        in_specs=[blk(0), blk(n_heads), blk(2 * n_heads), blk(0),
                  gate_spec, gate_spec,
                  pl.BlockSpec((1, HEAD_DIM), lambda b, h: (0, 0))],
        out_specs=pl.BlockSpec((seq, wide), lambda b, h: (b, h)),
        out_shape=jax.ShapeDtypeStruct((batch * seq, n_heads * HEAD_DIM), MXU_DTYPE),
        scratch_shapes=[
            per_chunk(HEAD_DIM, MXU_DTYPE),
            per_chunk(HEAD_DIM, MXU_DTYPE),
            per_chunk(HEAD_DIM, MXU_DTYPE),
            per_chunk(2 * HEAD_DIM, MXU_DTYPE),
            per_chunk(HEAD_DIM, MXU_DTYPE),
            per_chunk(c, MXU_DTYPE),
            per_chunk(HEAD_DIM, jnp.float32),
            pltpu.VMEM((heads, n_chunks, SUBLANES, HEAD_DIM), jnp.float32),
            per_chunk(HEAD_DIM, MXU_DTYPE),
            per_chunk(HEAD_DIM, jnp.float32),
            per_chunk(c, MXU_DTYPE),
            per_chunk(HEAD_DIM, jnp.float32),
        ],
        compiler_params=_cparams("parallel", "parallel"),
        name="gated_deltanet",
    )(qkv, qkv, qkv, zact, *gates, gain)


def kernel(x, ffn1_norm, ffn1_w_gate, ffn1_w_up, ffn1_w_down, mix_norm, w_in, conv_w, a_log, dt_bias,
           dn_norm, w_out, ffn2_norm, ffn2_w_gate, ffn2_w_up, ffn2_w_down, final_norm):
    batch, seq, d_model = x.shape
    n_heads = a_log.shape[0]
    d_head_group = n_heads * HEAD_DIM
    h = x.reshape(batch * seq, d_model)

    h, xn = ffn(h, ffn1_norm, ffn1_w_gate, ffn1_w_up, ffn1_w_down, mix_norm, True, MXU_DTYPE)

    wt = w_in.T
    qkv_attn = matmul_nt(xn, wt, 0, 3 * d_head_group, MXU_DTYPE)
    qkv_dn = proj_conv_qkv(xn, wt, conv_w, 3 * d_head_group, seq)
    zact = matmul_nt(xn, wt, 6 * d_head_group, d_head_group, jnp.float32, act=_silu)
    n_main = 7 * d_head_group
    rows_pad = lambda t: jnp.pad(t.astype(jnp.float32), ((0, HEAD_DIM - n_heads), (0, 0)))
    lane_pad = lambda t: jnp.pad(t.astype(jnp.float32).reshape(1, n_heads), ((0, 0), (0, HEAD_DIM - n_heads)))
    gates = dn_gates(xn, rows_pad(wt[n_main:n_main + n_heads]), rows_pad(wt[n_main + n_heads:]),
                     lane_pad(a_log), lane_pad(dt_bias))
    attn = dilated_attention(qkv_attn, batch, seq, n_heads)
    dn = gated_deltanet(qkv_dn, zact, gates, dn_norm.astype(jnp.float32).reshape(1, HEAD_DIM),
                        batch, seq, n_heads)
    h = matmul2_residual(attn, dn, w_out, h)

    out = ffn(h, ffn2_norm, ffn2_w_gate, ffn2_w_up, ffn2_w_down, final_norm, False, jnp.float32)
    return out.reshape(batch, seq, d_model)
```

```python
import functools
import math

import jax
import jax.numpy as jnp
from jax import lax
from jax.experimental import pallas as pl
from jax.experimental.pallas import tpu as pltpu

EPS = 1e-6
HEAD_DIM = 128
DILATED_CONFIGS = ((128, 1), (512, 4), (2048, 16))
CONV_WIDTH = 4
DN_CHUNK = 128
DN_HEADS_PER_STEP = 2
DN_CHUNKS_PER_ITER = 4
MASKED_BIAS = -1e30
ATTN_Q_SCALE = HEAD_DIM ** -0.5 * math.log2(math.e)
SUBLANES = 8

VMEM_LIMIT_BYTES = 56 * 1024 * 1024
FFN_VMEM_LIMIT_BYTES = 60 * 1024 * 1024
MXU_DTYPE = jnp.bfloat16


def _cparams(*sem):
    return pltpu.CompilerParams(dimension_semantics=sem, vmem_limit_bytes=VMEM_LIMIT_BYTES)


def _dot(a, b):
    return jnp.dot(a.astype(MXU_DTYPE), b.astype(MXU_DTYPE), preferred_element_type=jnp.float32)


def _dot_nt(a, b):
    return lax.dot_general(a.astype(MXU_DTYPE), b.astype(MXU_DTYPE), (((1,), (1,)), ((), ())),
                           preferred_element_type=jnp.float32)


def _silu(x):
    hx = 0.5 * x
    return hx + hx * jnp.tanh(hx)


PROJ_CONV_ROW_CHUNK = 512
FFN_ROW_CHUNK = 32
FFN_NORM_PIECES = 4
FFN_IN_CHUNKS = 8
FFN_K_CHUNK = 1024
FFN_N_CHUNK = 512


def _rmsnorm_rows(src_ref, w_ref, dst_ref, row0, n_rows):
    w = w_ref[...]
    step_rows = FFN_ROW_CHUNK * FFN_NORM_PIECES

    def body(r, carry):
        pieces = [pl.ds(pl.multiple_of(row0 + r * step_rows + p * FFN_ROW_CHUNK, FFN_ROW_CHUNK), FFN_ROW_CHUNK)
                  for p in range(FFN_NORM_PIECES)]
        sq = [src_ref[rows, :] for rows in pieces]
        inv = [lax.rsqrt(jnp.mean(x * x, axis=-1, keepdims=True) + EPS) for x in sq]
        for rows, scale in zip(pieces, inv):
            dst_ref[rows, :] = ((src_ref[rows, :] * scale) * w).astype(dst_ref.dtype)
        return carry
    lax.fori_loop(0, n_rows // step_rows, body, 0)


def _ffn_kernel(h_hbm, nw_ref, wg_ref, wu_ref, wd_ref, onw_ref, *rest, emit_res):
    if emit_res:
        res_hbm, norm_hbm, acc, xn_s, sem_in, sem_out = rest
    else:
        norm_hbm, acc, xn_s, sem_in, sem_out = rest
    i, j = pl.program_id(0), pl.program_id(1)
    n_i, n_j = pl.num_programs(0), pl.num_programs(1)
    bm, d = acc.shape
    rows_in = bm // FFN_IN_CHUNKS
    tile = lambda ref, t: ref.at[pl.ds(pl.multiple_of(t * bm, bm), bm), :]

    def in_copy(t, c):
        part = pl.ds(pl.multiple_of(c * rows_in, rows_in), rows_in)
        return pltpu.make_async_copy(tile(h_hbm, t).at[part, :], acc.at[part, :], sem_in.at[c])

    res_copy = lambda t: pltpu.make_async_copy(acc, tile(res_hbm, t), sem_out.at[0])
    norm_copy = lambda t: pltpu.make_async_copy(xn_s if emit_res else acc, tile(norm_hbm, t), sem_out.at[1])

    @pl.when(j == 0)
    def _():
        @pl.when(i > 0)
        def _():
            (res_copy if emit_res else norm_copy)(i - 1).wait()

        for c in range(FFN_IN_CHUNKS):
            in_copy(i, c).start()
        if emit_res:
            @pl.when(i > 0)
            def _():
                norm_copy(i - 1).wait()

        def land(c, carry):
            in_copy(i, c).wait()
            _rmsnorm_rows(acc, nw_ref, xn_s, c * rows_in, rows_in)
            return carry
        lax.fori_loop(0, FFN_IN_CHUNKS, land, 0)

    g = u = None
    for kc in range(d // FFN_K_CHUNK):
        ks = slice(kc * FFN_K_CHUNK, (kc + 1) * FFN_K_CHUNK)
        xk = xn_s[:, ks]
        gk, uk = _dot(xk, wg_ref[ks, :]), _dot(xk, wu_ref[ks, :])
        g, u = (gk, uk) if g is None else (g + gk, u + uk)
    act = ((0.5 * _silu(g)) * u).astype(MXU_DTYPE)
    for nc in range(d // FFN_N_CHUNK):
        ns = slice(nc * FFN_N_CHUNK, (nc + 1) * FFN_N_CHUNK)
        acc[:, ns] += _dot(act, wd_ref[:, ns])

    @pl.when(j == n_j - 1)
    def _():
        if emit_res:
            res_copy(i).start()
            _rmsnorm_rows(acc, onw_ref, xn_s, 0, bm)
        else:
            _rmsnorm_rows(acc, onw_ref, acc, 0, bm)
        norm_copy(i).start()

        @pl.when(i == n_i - 1)
        def _():
            if emit_res:
                res_copy(i).wait()
            norm_copy(i).wait()


def ffn(h, norm_w, w_gate, w_up, w_down, out_norm_w, emit_res, norm_dtype, bm=1024, bf=256):
    t, d = h.shape
    f = w_gate.shape[1]
    assert t % bm == 0 and f % bf == 0 and d % FFN_K_CHUNK == 0 and d % FFN_N_CHUNK == 0
    assert bm % (FFN_IN_CHUNKS * FFN_ROW_CHUNK * FFN_NORM_PIECES) == 0
    assert emit_res or norm_dtype == jnp.float32
    any_spec = pl.BlockSpec(memory_space=pl.ANY)
    row_spec = pl.BlockSpec((1, d), lambda i, j: (0, 0))
    norm_shape = jax.ShapeDtypeStruct((t, d), norm_dtype)
    return pl.pallas_call(
        functools.partial(_ffn_kernel, emit_res=emit_res),
        grid=(t // bm, f // bf),
        in_specs=[any_spec, row_spec,
                  pl.BlockSpec((d, bf), lambda i, j: (0, j)),
                  pl.BlockSpec((d, bf), lambda i, j: (0, j)),
                  pl.BlockSpec((bf, d), lambda i, j: (j, 0)),
                  row_spec],
        out_specs=[any_spec, any_spec] if emit_res else any_spec,
        out_shape=[jax.ShapeDtypeStruct((t, d), jnp.float32), norm_shape] if emit_res else norm_shape,
        scratch_shapes=[pltpu.VMEM((bm, d), jnp.float32), pltpu.VMEM((bm, d), MXU_DTYPE),
                        pltpu.SemaphoreType.DMA((FFN_IN_CHUNKS,)), pltpu.SemaphoreType.DMA((2,))],
        compiler_params=pltpu.CompilerParams(dimension_semantics=("arbitrary", "arbitrary"),
                                             vmem_limit_bytes=FFN_VMEM_LIMIT_BYTES),
        name="ffn",
    )(h, norm_w.reshape(1, d), w_gate, w_up, w_down, out_norm_w.reshape(1, d))


def _mm_kernel(a_ref, w_ref, o_ref, *, act, lead_blocks, lead_scale):
    acc = _dot_nt(a_ref[...], w_ref[...])
    if lead_blocks:
        acc = acc * jnp.where(pl.program_id(1) < lead_blocks, lead_scale, 1.0)
    o_ref[...] = (act(acc) if act else acc).astype(o_ref.dtype)


def matmul_nt(a, wt, row0, n, out_dtype, act=None, lead_cols=0, lead_scale=1.0, bm=1024, bn=512):
    t, k = a.shape
    assert row0 % bn == 0 and n % bn == 0 and lead_cols % bn == 0
    return pl.pallas_call(
        functools.partial(_mm_kernel, act=act, lead_blocks=lead_cols // bn, lead_scale=lead_scale),
        grid=(t // bm, n // bn),
        in_specs=[pl.BlockSpec((bm, k), lambda i, j: (i, 0)),
                  pl.BlockSpec((bn, k), lambda i, j: (row0 // bn + j, 0))],
        out_specs=pl.BlockSpec((bm, bn), lambda i, j: (i, j)),
        out_shape=jax.ShapeDtypeStruct((t, n), out_dtype),
        compiler_params=_cparams("parallel", "parallel"),
        name="matmul_nt",
    )(a, wt)


def _proj_conv_kernel(a_ref, w_ref, cw_ref, o_ref, acc_s, *, q_blocks, k_blocks):
    seq, bn = o_ref.shape
    chunk = min(PROJ_CONV_ROW_CHUNK, seq)
    w = w_ref[...].astype(MXU_DTYPE)
    cw = cw_ref[...]
    j = pl.program_id(1)
    is_v = j >= q_blocks + k_blocks
    qk_scale = jnp.where(j < q_blocks, HEAD_DIM ** -0.5, 1.0)
    row = lax.broadcasted_iota(jnp.int32, (SUBLANES, bn), 0)
    acc_s[:SUBLANES, :] = jnp.zeros((SUBLANES, bn), jnp.float32)
    for r in range(seq // chunk):
        acc_s[SUBLANES + r * chunk:SUBLANES + (r + 1) * chunk, :] = _dot_nt(a_ref[r * chunk:(r + 1) * chunk, :], w)
    for r in range(seq // chunk):
        y = None
        for i in range(CONV_WIDTH):
            start = SUBLANES + r * chunk - (CONV_WIDTH - 1 - i)
            term = cw[i:i + 1, :] * acc_s[start:start + chunk, :]
            y = term if y is None else y + term
        y = _silu(y)
        heads = []
        for g in range(bn // HEAD_DIM):
            yg = y[:, g * HEAD_DIM:(g + 1) * HEAD_DIM]
            inv = lax.rsqrt(jnp.sum(yg * yg, axis=-1, keepdims=True) + EPS) * qk_scale
            heads.append(yg * jnp.where(is_v, 1.0, inv))
        o_ref[r * chunk:(r + 1) * chunk, :] = jnp.concatenate(heads, axis=1)


def proj_conv_qkv(a, wt, conv_w, row0, seq, bn=256):
    t, k = a.shape
    n = conv_w.shape[1]
    width = n // 3
    assert row0 % bn == 0 and width % bn == 0 and bn % HEAD_DIM == 0
    return pl.pallas_call(
        functools.partial(_proj_conv_kernel, q_blocks=width // bn, k_blocks=width // bn),
        grid=(t // seq, n // bn),
        in_specs=[pl.BlockSpec((seq, k), lambda i, j: (i, 0)),
                  pl.BlockSpec((bn, k), lambda i, j: (row0 // bn + j, 0)),
                  pl.BlockSpec((CONV_WIDTH, bn), lambda i, j: (0, j))],
        out_specs=pl.BlockSpec((seq, bn), lambda i, j: (i, j)),
        out_shape=jax.ShapeDtypeStruct((t, n), jnp.float32),
        scratch_shapes=[pltpu.VMEM((SUBLANES + seq, bn), jnp.float32)],
        compiler_params=_cparams("parallel", "parallel"),
        name="proj_conv_qkv",
    )(a, wt, conv_w)


def _mm2_res_kernel(a1_ref, a2_ref, w1_ref, w2_ref, r_ref, o_ref):
    o_ref[...] = r_ref[...] + (_dot(a1_ref[...], w1_ref[...]) + _dot(a2_ref[...], w2_ref[...]))


def matmul2_residual(a1, a2, w, res, bm=1024, bn=512):
    t, k1 = a1.shape
    assert a2.shape[1] == k1 and w.shape[0] == 2 * k1
    n = w.shape[1]
    return pl.pallas_call(
        _mm2_res_kernel,
        grid=(t // bm, n // bn),
        in_specs=[pl.BlockSpec((bm, k1), lambda i, j: (i, 0)),
                  pl.BlockSpec((bm, k1), lambda i, j: (i, 0)),
                  pl.BlockSpec((k1, bn), lambda i, j: (0, j)),
                  pl.BlockSpec((k1, bn), lambda i, j: (1, j)),
                  pl.BlockSpec((bm, bn), lambda i, j: (i, j))],
        out_specs=pl.BlockSpec((bm, bn), lambda i, j: (i, j)),
        out_shape=jax.ShapeDtypeStruct((t, n), jnp.float32),
        compiler_params=_cparams("parallel", "parallel"),
        name="matmul2_residual",
    )(a1, a2, w, w, res)


def _attn_kernel(q_ref, k_ref, v_ref, o_ref, bias_ref, *, seq, tq):
    @pl.when((pl.program_id(0) == 0) & (pl.program_id(1) == 0))
    def _():
        i = lax.broadcasted_iota(jnp.int32, (tq, seq), 0)
        jj = lax.broadcasted_iota(jnp.int32, (tq, seq), 1)
        dist = i - (jj - (seq - tq))
        count = jnp.zeros((tq, seq), jnp.int32)
        for window, d in DILATED_CONFIGS:
            hit = (dist >= 0) & (dist <= (window // d) * d) & ((dist & (d - 1)) == 0)
            count = count + hit.astype(jnp.int32)
        bias_ref[...] = jnp.where(
            count == 3, math.log2(3.0),
            jnp.where(count == 2, 1.0, jnp.where(count == 1, 0.0, MASKED_BIAS)))

    k = k_ref[...].astype(MXU_DTYPE)
    v = v_ref[...].astype(MXU_DTYPE)
    for qi in range(seq // tq):
        kend = (qi + 1) * tq
        q = q_ref[qi * tq:kend, :]
        s = _dot_nt(q, k[:kend]) + bias_ref[:, seq - kend:]
        m = jnp.max(s, axis=-1, keepdims=True)
        p = jnp.exp2(s - m)
        den = jnp.sum(p, axis=-1, keepdims=True)
        o = _dot(p, v[:kend])
        o_ref[qi * tq:kend, :] = (o / den).astype(o_ref.dtype)


def dilated_attention(proj, batch, seq, n_heads, tq=256):
    tq = min(tq, seq)
    for window, d in DILATED_CONFIGS:
        assert d & (d - 1) == 0
    blk = lambda off: pl.BlockSpec((seq, HEAD_DIM), lambda b, h: (b, off + h))
    return pl.pallas_call(
        functools.partial(_attn_kernel, seq=seq, tq=tq),
        grid=(batch, n_heads),
        in_specs=[blk(0), blk(n_heads), blk(2 * n_heads)],
        out_specs=pl.BlockSpec((seq, HEAD_DIM), lambda b, h: (b, h)),
        out_shape=jax.ShapeDtypeStruct((batch * seq, n_heads * HEAD_DIM), MXU_DTYPE),
        scratch_shapes=[pltpu.VMEM((tq, seq), jnp.float32)],
        compiler_params=_cparams("arbitrary", "arbitrary"),
        name="dilated_attention",
    )(proj, proj, proj)


def _dn_gates_kernel(x_ref, wb_ref, wa_ref, alog_ref, dtb_ref, beta_ref, gc_ref):
    c = DN_CHUNK
    x = x_ref[...]
    beta_ref[...] = jax.nn.sigmoid(_dot_nt(x, wb_ref[...]))
    z = _dot_nt(x, wa_ref[...]) + dtb_ref[...]
    softplus = jnp.maximum(z, 0.0) + jnp.log1p(jnp.exp(-jnp.abs(z)))
    gc = -jnp.exp(alog_ref[...]) * softplus
    row = lax.broadcasted_iota(jnp.int32, gc.shape, 0)
    s = 1
    while s < c:
        gc = gc + jnp.where((row & (c - 1)) >= s, pltpu.roll(gc, s, axis=0), 0.0)
        s *= 2
    gc_ref[...] = gc


def dn_gates(xn, wt_beta, wt_decay, a_row, dt_row, bm=1024):
    t, d = xn.shape
    out = jax.ShapeDtypeStruct((t, HEAD_DIM), jnp.float32)
    tok = pl.BlockSpec((bm, HEAD_DIM), lambda i: (i, 0))
    wspec = pl.BlockSpec((HEAD_DIM, d), lambda i: (0, 0))
    row_spec = pl.BlockSpec((1, HEAD_DIM), lambda i: (0, 0))
    return pl.pallas_call(
        _dn_gates_kernel,
        grid=(t // bm,),
        in_specs=[pl.BlockSpec((bm, d), lambda i: (i, 0)), wspec, wspec, row_spec, row_spec],
        out_specs=[tok, tok],
        out_shape=[out, out],
        compiler_params=_cparams("parallel"),
        name="dn_gates",
    )(xn, wt_beta, wt_decay, a_row, dt_row)


def _lane_column(x, lane, col):
    picked = jnp.sum(jnp.where(lane == col, x, 0.0), axis=-1, keepdims=True)
    return jnp.broadcast_to(picked, x.shape)


def _unit_lower_inverses(nmats, ci, cj):
    c = nmats[0].shape[0]
    eye = (ci == cj).astype(jnp.float32)
    same8 = (ci >> 3) == (cj >> 3)
    ms = [-jnp.where(same8, nmat, 0.0) for nmat in nmats]
    tinvs = [eye + m for m in ms]
    ms = [_dot(m, m) for m in ms]
    yield
    xs = [_dot(m, jnp.concatenate([m, tinv], axis=1)) for m, tinv in zip(ms, tinvs)]
    yield
    tinvs = [tinv + x[:, c:] for tinv, x in zip(tinvs, xs)]
    ys = [_dot(x[:, :c], tinv) for x, tinv in zip(xs, tinvs)]
    yield
    tinvs = [tinv + y for tinv, y in zip(tinvs, ys)]
    shift = 3
    while (1 << shift) < c:
        pick = ((ci >> (shift + 1)) == (cj >> (shift + 1))) & ((ci >> shift) != (cj >> shift))
        ys = [_dot(jnp.where(pick, nmat, 0.0), tinv) for nmat, tinv in zip(nmats, tinvs)]
        yield
        ys = [_dot(tinv, y) for tinv, y in zip(tinvs, ys)]
        yield
        tinvs = [tinv - y for tinv, y in zip(tinvs, ys)]
        shift += 1
    return tinvs


def _interleave(first, second):
    results = [None, None]
    live = [first, second]
    while any(g is not None for g in live):
        for idx, g in enumerate(live):
            if g is None:
                continue
            try:
                next(g)
            except StopIteration as stop:
                results[idx] = stop.value
                live[idx] = None
    return results


def _deltanet_kernel(q_ref, k_ref, v_ref, z_ref, beta_ref, gc_ref, gain_ref, o_ref,
                     q_s, k_s, kb_s, rhs_s, qd_s, kdt_s, gc_s, egl_s, w_s, u_s, qkd_s, o_s, *, seq):
    c = DN_CHUNK
    hd = HEAD_DIM
    n_chunks = seq // c
    group = DN_CHUNKS_PER_ITER
    heads = DN_HEADS_PER_STEP
    lane = lax.broadcasted_iota(jnp.int32, (seq, hd), 1)

    for e in range(heads):
        cols = slice(e * hd, (e + 1) * hd)
        head = pl.program_id(1) * heads + e
        beta = _lane_column(beta_ref[...], lane, head)
        gc = _lane_column(gc_ref[...], lane, head).reshape(n_chunks, c, hd)
        eg = jnp.exp(gc)
        kd = jnp.exp(gc[:, c - 1:c, :] - gc)
        q = q_ref[:, cols]
        k = k_ref[:, cols]
        kb = k * beta
        kdt = (k * kd.reshape(seq, hd)).T
        eg2 = eg.reshape(seq, hd)
        q_s[e] = q.astype(q_s.dtype).reshape(n_chunks, c, hd)
        k_s[e] = k.astype(k_s.dtype).reshape(n_chunks, c, hd)
        kb_s[e] = kb.astype(kb_s.dtype).reshape(n_chunks, c, hd)
        rhs_s[e] = jnp.concatenate([kb * eg2, v_ref[:, cols] * beta], axis=1
                                   ).astype(rhs_s.dtype).reshape(n_chunks, c, 2 * hd)
        qd_s[e] = (q * eg2).astype(qd_s.dtype).reshape(n_chunks, c, hd)
        gc_s[e] = gc
        egl_s[e] = eg[:, c - SUBLANES:, :]
        for n in range(n_chunks):
            kdt_s[e, n] = kdt[:, n * c:(n + 1) * c].astype(kdt_s.dtype)

    ci = lax.broadcasted_iota(jnp.int32, (c, c), 0)
    cj = lax.broadcasted_iota(jnp.int32, (c, c), 1)

    def prep_stages(it):
        chains = [(e, it * group + g) for g in range(group) for e in range(heads)]
        ks = [k_s[e, n] for e, n in chains]
        kbs = [kb_s[e, n] for e, n in chains]
        qs = [q_s[e, n] for e, n in chains]
        gcs = [gc_s[e, n] for e, n in chains]
        rhss = [rhs_s[e, n] for e, n in chains]
        yield
        kks = [_dot_nt(kb, k) for kb, k in zip(kbs, ks)]
        qks = [_dot_nt(q, k) for q, k in zip(qs, ks)]
        yield
        decays = [jnp.exp(jnp.where(ci >= cj, gc - gc.T, -jnp.inf)) for gc in gcs]
        nmats = [jnp.where(ci > cj, kk * decay, 0.0) for kk, decay in zip(kks, decays)]
        tinvs = yield from _unit_lower_inverses(nmats, ci, cj)
        sols = [_dot(tinv, rhs) for tinv, rhs in zip(tinvs, rhss)]
        yield
        return [(e, n, (qk * decay).astype(qkd_s.dtype), sol[:, :hd].astype(w_s.dtype), sol[:, hd:])
                for (e, n), qk, decay, sol in zip(chains, qks, decays, sols)]

    def store_prepped(prepped):
        for e, n, qkd, w, u in prepped:
            qkd_s[e, n] = qkd
            w_s[e, n] = w
            u_s[e, n] = u

    def scan_stages(it, states):
        r = range(heads)
        loaded = []
        for g in range(group):
            n = it * group + g
            loaded.append(([u_s[e, n] for e in r], [w_s[e, n] for e in r], [qd_s[e, n] for e in r],
                           [qkd_s[e, n] for e in r], [kdt_s[e, n] for e in r],
                           [egl_s[e, n][SUBLANES - 1:SUBLANES, :] for e in r]))
        yield
        outs = []
        for us, ws, qds, qkds, kdts, egls in loaded:
            v_news = [us[e] - _dot(ws[e], states[e]) for e in r]
            from_state = [_dot(qds[e], states[e]) for e in r]
            yield
            outs.append([from_state[e] + _dot(qkds[e], v_news[e]) for e in r])
            states = tuple(states[e] * egls[e] + _dot(kdts[e], v_news[e]) for e in r)
            yield
        return states, outs

    def store_outs(it, outs):
        for g, chunk_outs in enumerate(outs):
            for e in range(heads):
                o_s[e, it * group + g] = chunk_outs[e]

    def run(gen):
        return _interleave(gen, None)[0]

    n_groups = n_chunks // group
    store_prepped(run(prep_stages(0)))

    def body(it, states):
        prepped, (states, outs) = _interleave(prep_stages(it), scan_stages(it - 1, states))
        store_prepped(prepped)
        store_outs(it - 1, outs)
        return states

    states = lax.fori_loop(1, n_groups, body, tuple(jnp.zeros((hd, hd), jnp.float32) for _ in range(heads)))
    _, outs = run(scan_stages(n_groups - 1, states))
    store_outs(n_groups - 1, outs)

    for e in range(heads):
        cols = slice(e * hd, (e + 1) * hd)
        o = o_s[e].reshape(seq, hd)
        o = o * lax.rsqrt(jnp.mean(o * o, axis=-1, keepdims=True) + EPS) * gain_ref[...]
        o_ref[:, cols] = (o * z_ref[:, cols]).astype(o_ref.dtype)


def gated_deltanet(qkv, zact, gates, gain, batch, seq, n_heads):
    heads = DN_HEADS_PER_STEP
    c = DN_CHUNK
    n_chunks = seq // c
    assert n_heads % heads == 0 and n_chunks % DN_CHUNKS_PER_ITER == 0
    wide = heads * HEAD_DIM
    blk = lambda off: pl.BlockSpec((seq, wide), lambda b, h: (b, off // heads + h))
    gate_spec = pl.BlockSpec((seq, HEAD_DIM), lambda b, h: (b, 0))
    per_chunk = lambda width, dtype: pltpu.VMEM((heads, n_chunks, c, width), dtype)
    return pl.pallas_call(
        functools.partial(_deltanet_kernel, seq=seq),
        grid=(batch, n_heads // heads),
        in_specs=[blk(0), blk(n_heads), blk(2 * n_heads), blk(0),
                  gate_spec, gate_spec,
                  pl.BlockSpec((1, HEAD_DIM), lambda b, h: (0, 0))],
        out_specs=pl.BlockSpec((seq, wide), lambda b, h: (b, h)),
        out_shape=jax.ShapeDtypeStruct((batch * seq, n_heads * HEAD_DIM), MXU_DTYPE),
        scratch_shapes=[
            per_chunk(HEAD_DIM, MXU_DTYPE),
            per_chunk(HEAD_DIM, MXU_DTYPE),
            per_chunk(HEAD_DIM, MXU_DTYPE),
            per_chunk(2 * HEAD_DIM, MXU_DTYPE),
            per_chunk(HEAD_DIM, MXU_DTYPE),
            per_chunk(c, MXU_DTYPE),
            per_chunk(HEAD_DIM, jnp.float32),
            pltpu.VMEM((heads, n_chunks, SUBLANES, HEAD_DIM), jnp.float32),
            per_chunk(HEAD_DIM, MXU_DTYPE),
            per_chunk(HEAD_DIM, jnp.float32),
            per_chunk(c, MXU_DTYPE),
            per_chunk(HEAD_DIM, jnp.float32),
        ],
        compiler_params=_cparams("parallel", "parallel"),
        name="gated_deltanet",
    )(qkv, qkv, qkv, zact, *gates, gain)


def kernel(x, ffn1_norm, ffn1_w_gate, ffn1_w_up, ffn1_w_down, mix_norm, w_in, conv_w, a_log, dt_bias,
           dn_norm, w_out, ffn2_norm, ffn2_w_gate, ffn2_w_up, ffn2_w_down, final_norm):
    batch, seq, d_model = x.shape
    n_heads = a_log.shape[0]
    d_head_group = n_heads * HEAD_DIM
    h = x.reshape(batch * seq, d_model)

    h, xn = ffn(h, ffn1_norm, ffn1_w_gate, ffn1_w_up, ffn1_w_down, mix_norm, True, MXU_DTYPE)

    wt = w_in.T
    qkv_attn = matmul_nt(xn, wt, 0, 3 * d_head_group, MXU_DTYPE, lead_cols=d_head_group, lead_scale=ATTN_Q_SCALE)
    qkv_dn = proj_conv_qkv(xn, wt, conv_w, 3 * d_head_group, seq)
    zact = matmul_nt(xn, wt, 6 * d_head_group, d_head_group, jnp.float32, act=_silu)
    n_main = 7 * d_head_group
    rows_pad = lambda t: jnp.pad(t.astype(jnp.float32), ((0, HEAD_DIM - n_heads), (0, 0)))
    lane_pad = lambda t: jnp.pad(t.astype(jnp.float32).reshape(1, n_heads), ((0, 0), (0, HEAD_DIM - n_heads)))
    gates = dn_gates(xn, rows_pad(wt[n_main:n_main + n_heads]), rows_pad(wt[n_main + n_heads:]),
                     lane_pad(a_log), lane_pad(dt_bias))
    attn = dilated_attention(qkv_attn, batch, seq, n_heads)
    dn = gated_deltanet(qkv_dn, zact, gates, dn_norm.astype(jnp.float32).reshape(1, HEAD_DIM),
                        batch, seq, n_heads)
    h = matmul2_residual(attn, dn, w_out, h)

    out = ffn(h, ffn2_norm, ffn2_w_gate, ffn2_w_up, ffn2_w_down, final_norm, False, jnp.float32)
    return out.reshape(batch, seq, d_model)
```

```python
import functools
import math

import jax
import jax.numpy as jnp
from jax import lax
from jax.experimental import pallas as pl
from jax.experimental.pallas import tpu as pltpu

EPS = 1e-6
HEAD_DIM = 128
DILATED_CONFIGS = ((128, 1), (512, 4), (2048, 16))
CONV_WIDTH = 4
DN_CHUNK = 128
DN_HEADS_PER_STEP = 2
DN_CHUNKS_PER_ITER = 8
MASKED_BIAS = -1e30
ATTN_Q_SCALE = HEAD_DIM ** -0.5 * math.log2(math.e)
SUBLANES = 8

VMEM_LIMIT_BYTES = 56 * 1024 * 1024
FFN_VMEM_LIMIT_BYTES = 60 * 1024 * 1024
MXU_DTYPE = jnp.bfloat16


def _cparams(*sem):
    return pltpu.CompilerParams(dimension_semantics=sem, vmem_limit_bytes=VMEM_LIMIT_BYTES)


def _dot(a, b):
    return jnp.dot(a.astype(MXU_DTYPE), b.astype(MXU_DTYPE), preferred_element_type=jnp.float32)


def _dot_nt(a, b):
    return lax.dot_general(a.astype(MXU_DTYPE), b.astype(MXU_DTYPE), (((1,), (1,)), ((), ())),
                           preferred_element_type=jnp.float32)


def _silu(x):
    hx = 0.5 * x
    return hx + hx * jnp.tanh(hx)


PROJ_CONV_ROW_CHUNK = 512
FFN_ROW_CHUNK = 32
FFN_NORM_PIECES = 4
FFN_IN_CHUNKS = 8
FFN_K_CHUNK = 1024
FFN_N_CHUNK = 512


def _rmsnorm_rows(src_ref, w_ref, dst_ref, row0, n_rows):
    w = w_ref[...]
    step_rows = FFN_ROW_CHUNK * FFN_NORM_PIECES

    def body(r, carry):
        pieces = [pl.ds(pl.multiple_of(row0 + r * step_rows + p * FFN_ROW_CHUNK, FFN_ROW_CHUNK), FFN_ROW_CHUNK)
                  for p in range(FFN_NORM_PIECES)]
        sq = [src_ref[rows, :] for rows in pieces]
        inv = [lax.rsqrt(jnp.mean(x * x, axis=-1, keepdims=True) + EPS) for x in sq]
        for rows, scale in zip(pieces, inv):
            dst_ref[rows, :] = ((src_ref[rows, :] * scale) * w).astype(dst_ref.dtype)
        return carry
    lax.fori_loop(0, n_rows // step_rows, body, 0)


def _ffn_kernel(h_hbm, nw_ref, wg_ref, wu_ref, wd_ref, onw_ref, *rest, emit_res):
    if emit_res:
        res_hbm, norm_hbm, acc, xn_s, sem_in, sem_out = rest
    else:
        norm_hbm, acc, xn_s, sem_in, sem_out = rest
    i, j = pl.program_id(0), pl.program_id(1)
    n_i, n_j = pl.num_programs(0), pl.num_programs(1)
    bm, d = acc.shape
    rows_in = bm // FFN_IN_CHUNKS
    tile = lambda ref, t: ref.at[pl.ds(pl.multiple_of(t * bm, bm), bm), :]

    def in_copy(t, c):
        part = pl.ds(pl.multiple_of(c * rows_in, rows_in), rows_in)
        return pltpu.make_async_copy(tile(h_hbm, t).at[part, :], acc.at[part, :], sem_in.at[c])

    res_copy = lambda t: pltpu.make_async_copy(acc, tile(res_hbm, t), sem_out.at[0])
    norm_copy = lambda t: pltpu.make_async_copy(xn_s if emit_res else acc, tile(norm_hbm, t), sem_out.at[1])

    @pl.when(j == 0)
    def _():
        @pl.when(i > 0)
        def _():
            (res_copy if emit_res else norm_copy)(i - 1).wait()

        for c in range(FFN_IN_CHUNKS):
            in_copy(i, c).start()
        if emit_res:
            @pl.when(i > 0)
            def _():
                norm_copy(i - 1).wait()

        def land(c, carry):
            in_copy(i, c).wait()
            _rmsnorm_rows(acc, nw_ref, xn_s, c * rows_in, rows_in)
            return carry
        lax.fori_loop(0, FFN_IN_CHUNKS, land, 0)

    g = u = None
    for kc in range(d // FFN_K_CHUNK):
        ks = slice(kc * FFN_K_CHUNK, (kc + 1) * FFN_K_CHUNK)
        xk = xn_s[:, ks]
        gk, uk = _dot(xk, wg_ref[ks, :]), _dot(xk, wu_ref[ks, :])
        g, u = (gk, uk) if g is None else (g + gk, u + uk)
    act = ((0.5 * _silu(g)) * u).astype(MXU_DTYPE)
    for nc in range(d // FFN_N_CHUNK):
        ns = slice(nc * FFN_N_CHUNK, (nc + 1) * FFN_N_CHUNK)
        acc[:, ns] += _dot(act, wd_ref[:, ns])

    @pl.when(j == n_j - 1)
    def _():
        if emit_res:
            res_copy(i).start()
            _rmsnorm_rows(acc, onw_ref, xn_s, 0, bm)
        else:
            _rmsnorm_rows(acc, onw_ref, acc, 0, bm)
        norm_copy(i).start()

        @pl.when(i == n_i - 1)
        def _():
            if emit_res:
                res_copy(i).wait()
            norm_copy(i).wait()


def ffn(h, norm_w, w_gate, w_up, w_down, out_norm_w, emit_res, norm_dtype, bm=1024, bf=256):
    t, d = h.shape
    f = w_gate.shape[1]
    assert t % bm == 0 and f % bf == 0 and d % FFN_K_CHUNK == 0 and d % FFN_N_CHUNK == 0
    assert bm % (FFN_IN_CHUNKS * FFN_ROW_CHUNK * FFN_NORM_PIECES) == 0
    assert emit_res or norm_dtype == jnp.float32
    any_spec = pl.BlockSpec(memory_space=pl.ANY)
    row_spec = pl.BlockSpec((1, d), lambda i, j: (0, 0))
    norm_shape = jax.ShapeDtypeStruct((t, d), norm_dtype)
    return pl.pallas_call(
        functools.partial(_ffn_kernel, emit_res=emit_res),
        grid=(t // bm, f // bf),
        in_specs=[any_spec, row_spec,
                  pl.BlockSpec((d, bf), lambda i, j: (0, j)),
                  pl.BlockSpec((d, bf), lambda i, j: (0, j)),
                  pl.BlockSpec((bf, d), lambda i, j: (j, 0)),
                  row_spec],
        out_specs=[any_spec, any_spec] if emit_res else any_spec,
        out_shape=[jax.ShapeDtypeStruct((t, d), jnp.float32), norm_shape] if emit_res else norm_shape,
        scratch_shapes=[pltpu.VMEM((bm, d), jnp.float32), pltpu.VMEM((bm, d), MXU_DTYPE),
                        pltpu.SemaphoreType.DMA((FFN_IN_CHUNKS,)), pltpu.SemaphoreType.DMA((2,))],
        compiler_params=pltpu.CompilerParams(dimension_semantics=("arbitrary", "arbitrary"),
                                             vmem_limit_bytes=FFN_VMEM_LIMIT_BYTES),
        name="ffn",
    )(h, norm_w.reshape(1, d), w_gate, w_up, w_down, out_norm_w.reshape(1, d))


def _mm_kernel(a_ref, w_ref, o_ref, *, act, lead_blocks, lead_scale):
    acc = _dot_nt(a_ref[...], w_ref[...])
    if lead_blocks:
        acc = acc * jnp.where(pl.program_id(1) < lead_blocks, lead_scale, 1.0)
    o_ref[...] = (act(acc) if act else acc).astype(o_ref.dtype)


def matmul_nt(a, wt, row0, n, out_dtype, act=None, lead_cols=0, lead_scale=1.0, bm=1024, bn=512):
    t, k = a.shape
    assert row0 % bn == 0 and n % bn == 0 and lead_cols % bn == 0
    return pl.pallas_call(
        functools.partial(_mm_kernel, act=act, lead_blocks=lead_cols // bn, lead_scale=lead_scale),
        grid=(t // bm, n // bn),
        in_specs=[pl.BlockSpec((bm, k), lambda i, j: (i, 0)),
                  pl.BlockSpec((bn, k), lambda i, j: (row0 // bn + j, 0))],
        out_specs=pl.BlockSpec((bm, bn), lambda i, j: (i, j)),
        out_shape=jax.ShapeDtypeStruct((t, n), out_dtype),
        compiler_params=_cparams("parallel", "parallel"),
        name="matmul_nt",
    )(a, wt)


def _proj_conv_kernel(a_ref, w_ref, cw_ref, o_ref, acc_s, *, q_blocks, k_blocks):
    seq, bn = o_ref.shape
    chunk = min(PROJ_CONV_ROW_CHUNK, seq)
    w = w_ref[...].astype(MXU_DTYPE)
    cw = cw_ref[...]
    j = pl.program_id(1)
    is_v = j >= q_blocks + k_blocks
    qk_scale = jnp.where(j < q_blocks, HEAD_DIM ** -0.5, 1.0)
    row = lax.broadcasted_iota(jnp.int32, (SUBLANES, bn), 0)
    acc_s[:SUBLANES, :] = jnp.zeros((SUBLANES, bn), jnp.float32)
    for r in range(seq // chunk):
        acc_s[SUBLANES + r * chunk:SUBLANES + (r + 1) * chunk, :] = _dot_nt(a_ref[r * chunk:(r + 1) * chunk, :], w)
    for r in range(seq // chunk):
        y = None
        for i in range(CONV_WIDTH):
            start = SUBLANES + r * chunk - (CONV_WIDTH - 1 - i)
            term = cw[i:i + 1, :] * acc_s[start:start + chunk, :]
            y = term if y is None else y + term
        y = _silu(y)
        heads = []
        for g in range(bn // HEAD_DIM):
            yg = y[:, g * HEAD_DIM:(g + 1) * HEAD_DIM]
            inv = lax.rsqrt(jnp.sum(yg * yg, axis=-1, keepdims=True) + EPS) * qk_scale
            heads.append(yg * jnp.where(is_v, 1.0, inv))
        o_ref[r * chunk:(r + 1) * chunk, :] = jnp.concatenate(heads, axis=1)


def proj_conv_qkv(a, wt, conv_w, row0, seq, bn=256):
    t, k = a.shape
    n = conv_w.shape[1]
    width = n // 3
    assert row0 % bn == 0 and width % bn == 0 and bn % HEAD_DIM == 0
    return pl.pallas_call(
        functools.partial(_proj_conv_kernel, q_blocks=width // bn, k_blocks=width // bn),
        grid=(t // seq, n // bn),
        in_specs=[pl.BlockSpec((seq, k), lambda i, j: (i, 0)),
                  pl.BlockSpec((bn, k), lambda i, j: (row0 // bn + j, 0)),
                  pl.BlockSpec((CONV_WIDTH, bn), lambda i, j: (0, j))],
        out_specs=pl.BlockSpec((seq, bn), lambda i, j: (i, j)),
        out_shape=jax.ShapeDtypeStruct((t, n), jnp.float32),
        scratch_shapes=[pltpu.VMEM((SUBLANES + seq, bn), jnp.float32)],
        compiler_params=_cparams("parallel", "parallel"),
        name="proj_conv_qkv",
    )(a, wt, conv_w)


def _mm2_res_kernel(a1_ref, a2_ref, w1_ref, w2_ref, r_ref, o_ref):
    o_ref[...] = r_ref[...] + (_dot(a1_ref[...], w1_ref[...]) + _dot(a2_ref[...], w2_ref[...]))


def matmul2_residual(a1, a2, w, res, bm=1024, bn=512):
    t, k1 = a1.shape
    assert a2.shape[1] == k1 and w.shape[0] == 2 * k1
    n = w.shape[1]
    return pl.pallas_call(
        _mm2_res_kernel,
        grid=(t // bm, n // bn),
        in_specs=[pl.BlockSpec((bm, k1), lambda i, j: (i, 0)),
                  pl.BlockSpec((bm, k1), lambda i, j: (i, 0)),
                  pl.BlockSpec((k1, bn), lambda i, j: (0, j)),
                  pl.BlockSpec((k1, bn), lambda i, j: (1, j)),
                  pl.BlockSpec((bm, bn), lambda i, j: (i, j))],
        out_specs=pl.BlockSpec((bm, bn), lambda i, j: (i, j)),
        out_shape=jax.ShapeDtypeStruct((t, n), jnp.float32),
        compiler_params=_cparams("parallel", "parallel"),
        name="matmul2_residual",
    )(a1, a2, w, w, res)


def _attn_kernel(q_ref, k_ref, v_ref, o_ref, bias_ref, *, seq, tq):
    @pl.when((pl.program_id(0) == 0) & (pl.program_id(1) == 0))
    def _():
        i = lax.broadcasted_iota(jnp.int32, (tq, seq), 0)
        jj = lax.broadcasted_iota(jnp.int32, (tq, seq), 1)
        dist = i - (jj - (seq - tq))
        count = jnp.zeros((tq, seq), jnp.int32)
        for window, d in DILATED_CONFIGS:
            hit = (dist >= 0) & (dist <= (window // d) * d) & ((dist & (d - 1)) == 0)
            count = count + hit.astype(jnp.int32)
        bias_ref[...] = jnp.where(
            count == 3, math.log2(3.0),
            jnp.where(count == 2, 1.0, jnp.where(count == 1, 0.0, MASKED_BIAS)))

    k = k_ref[...].astype(MXU_DTYPE)
    v = v_ref[...].astype(MXU_DTYPE)
    n_tiles = seq // tq

    def scores(qi):
        kend = (qi + 1) * tq
        return _dot_nt(q_ref[qi * tq:kend, :], k[:kend]) + bias_ref[:, seq - kend:]

    s = scores(0)
    for qi in range(n_tiles):
        kend = (qi + 1) * tq
        s_next = scores(qi + 1) if qi + 1 < n_tiles else None
        m = jnp.max(s, axis=-1, keepdims=True)
        p = jnp.exp2(s - m)
        den = jnp.sum(p, axis=-1, keepdims=True)
        o = _dot(p, v[:kend])
        o_ref[qi * tq:kend, :] = (o / den).astype(o_ref.dtype)
        s = s_next


def dilated_attention(proj, batch, seq, n_heads, tq=256):
    tq = min(tq, seq)
    for window, d in DILATED_CONFIGS:
        assert d & (d - 1) == 0
    blk = lambda off: pl.BlockSpec((seq, HEAD_DIM), lambda b, h: (b, off + h))
    return pl.pallas_call(
        functools.partial(_attn_kernel, seq=seq, tq=tq),
        grid=(batch, n_heads),
        in_specs=[blk(0), blk(n_heads), blk(2 * n_heads)],
        out_specs=pl.BlockSpec((seq, HEAD_DIM), lambda b, h: (b, h)),
        out_shape=jax.ShapeDtypeStruct((batch * seq, n_heads * HEAD_DIM), MXU_DTYPE),
        scratch_shapes=[pltpu.VMEM((tq, seq), jnp.float32)],
        compiler_params=_cparams("arbitrary", "arbitrary"),
        name="dilated_attention",
    )(proj, proj, proj)


def _dn_gates_kernel(x_ref, wb_ref, wa_ref, alog_ref, dtb_ref, beta_ref, gc_ref):
    c = DN_CHUNK
    x = x_ref[...]
    beta_ref[...] = jax.nn.sigmoid(_dot_nt(x, wb_ref[...]))
    z = _dot_nt(x, wa_ref[...]) + dtb_ref[...]
    softplus = jnp.maximum(z, 0.0) + jnp.log1p(jnp.exp(-jnp.abs(z)))
    gc = -jnp.exp(alog_ref[...]) * softplus
    row = lax.broadcasted_iota(jnp.int32, gc.shape, 0)
    s = 1
    while s < c:
        gc = gc + jnp.where((row & (c - 1)) >= s, pltpu.roll(gc, s, axis=0), 0.0)
        s *= 2
    gc_ref[...] = gc


def dn_gates(xn, wt_beta, wt_decay, a_row, dt_row, bm=1024):
    t, d = xn.shape
    out = jax.ShapeDtypeStruct((t, HEAD_DIM), jnp.float32)
    tok = pl.BlockSpec((bm, HEAD_DIM), lambda i: (i, 0))
    wspec = pl.BlockSpec((HEAD_DIM, d), lambda i: (0, 0))
    row_spec = pl.BlockSpec((1, HEAD_DIM), lambda i: (0, 0))
    return pl.pallas_call(
        _dn_gates_kernel,
        grid=(t // bm,),
        in_specs=[pl.BlockSpec((bm, d), lambda i: (i, 0)), wspec, wspec, row_spec, row_spec],
        out_specs=[tok, tok],
        out_shape=[out, out],
        compiler_params=_cparams("parallel"),
        name="dn_gates",
    )(xn, wt_beta, wt_decay, a_row, dt_row)


def _lane_column(x, lane, col):
    picked = jnp.sum(jnp.where(lane == col, x, 0.0), axis=-1, keepdims=True)
    return jnp.broadcast_to(picked, x.shape)


def _unit_lower_inverses(nmats, ci, cj):
    c = nmats[0].shape[0]
    eye = (ci == cj).astype(jnp.float32)
    same8 = (ci >> 3) == (cj >> 3)
    ms = [-jnp.where(same8, nmat, 0.0) for nmat in nmats]
    tinvs = [eye + m for m in ms]
    ms = [_dot(m, m) for m in ms]
    yield
    xs = [_dot(m, jnp.concatenate([m, tinv], axis=1)) for m, tinv in zip(ms, tinvs)]
    yield
    tinvs = [tinv + x[:, c:] for tinv, x in zip(tinvs, xs)]
    ys = [_dot(x[:, :c], tinv) for x, tinv in zip(xs, tinvs)]
    yield
    tinvs = [tinv + y for tinv, y in zip(tinvs, ys)]
    shift = 3
    while (1 << shift) < c:
        pick = ((ci >> (shift + 1)) == (cj >> (shift + 1))) & ((ci >> shift) != (cj >> shift))
        ys = [_dot(jnp.where(pick, nmat, 0.0), tinv) for nmat, tinv in zip(nmats, tinvs)]
        yield
        ys = [_dot(tinv, y) for tinv, y in zip(tinvs, ys)]
        yield
        tinvs = [tinv - y for tinv, y in zip(tinvs, ys)]
        shift += 1
    return tinvs


def _interleave(first, second):
    results = [None, None]
    live = [first, second]
    while any(g is not None for g in live):
        for idx, g in enumerate(live):
            if g is None:
                continue
            try:
                next(g)
            except StopIteration as stop:
                results[idx] = stop.value
                live[idx] = None
    return results


def _deltanet_kernel(q_ref, k_ref, v_ref, z_ref, beta_ref, gc_ref, gain_ref, o_ref,
                     q_s, k_s, kb_s, rhs_s, qd_s, kdt_s, gc_s, egl_s, w_s, u_s, qkd_s, o_s, *, seq):
    c = DN_CHUNK
    hd = HEAD_DIM
    n_chunks = seq // c
    group = DN_CHUNKS_PER_ITER
    heads = DN_HEADS_PER_STEP
    lane = lax.broadcasted_iota(jnp.int32, (seq, hd), 1)

    for e in range(heads):
        cols = slice(e * hd, (e + 1) * hd)
        head = pl.program_id(1) * heads + e
        beta = _lane_column(beta_ref[...], lane, head)
        gc = _lane_column(gc_ref[...], lane, head).reshape(n_chunks, c, hd)
        eg = jnp.exp(gc)
        kd = jnp.exp(gc[:, c - 1:c, :] - gc)
        q = q_ref[:, cols]
        k = k_ref[:, cols]
        kb = k * beta
        kdt = (k * kd.reshape(seq, hd)).T
        eg2 = eg.reshape(seq, hd)
        q_s[e] = q.astype(q_s.dtype).reshape(n_chunks, c, hd)
        k_s[e] = k.astype(k_s.dtype).reshape(n_chunks, c, hd)
        kb_s[e] = kb.astype(kb_s.dtype).reshape(n_chunks, c, hd)
        rhs_s[e] = jnp.concatenate([kb * eg2, v_ref[:, cols] * beta], axis=1
                                   ).astype(rhs_s.dtype).reshape(n_chunks, c, 2 * hd)
        qd_s[e] = (q * eg2).astype(qd_s.dtype).reshape(n_chunks, c, hd)
        gc_s[e] = gc
        egl_s[e] = eg[:, c - SUBLANES:, :]
        for n in range(n_chunks):
            kdt_s[e, n] = kdt[:, n * c:(n + 1) * c].astype(kdt_s.dtype)

    ci = lax.broadcasted_iota(jnp.int32, (c, c), 0)
    cj = lax.broadcasted_iota(jnp.int32, (c, c), 1)

    def prep_stages(it):
        chains = [(e, it * group + g) for g in range(group) for e in range(heads)]
        ks = [k_s[e, n] for e, n in chains]
        kbs = [kb_s[e, n] for e, n in chains]
        qs = [q_s[e, n] for e, n in chains]
        gcs = [gc_s[e, n] for e, n in chains]
        rhss = [rhs_s[e, n] for e, n in chains]
        yield
        kks = [_dot_nt(kb, k) for kb, k in zip(kbs, ks)]
        qks = [_dot_nt(q, k) for q, k in zip(qs, ks)]
        yield
        decays = [jnp.exp(jnp.where(ci >= cj, gc - gc.T, -jnp.inf)) for gc in gcs]
        nmats = [jnp.where(ci > cj, kk * decay, 0.0) for kk, decay in zip(kks, decays)]
        tinvs = yield from _unit_lower_inverses(nmats, ci, cj)
        sols = [_dot(tinv, rhs) for tinv, rhs in zip(tinvs, rhss)]
        yield
        return [(e, n, (qk * decay).astype(qkd_s.dtype), sol[:, :hd].astype(w_s.dtype), sol[:, hd:])
                for (e, n), qk, decay, sol in zip(chains, qks, decays, sols)]

    def store_prepped(prepped):
        for e, n, qkd, w, u in prepped:
            qkd_s[e, n] = qkd
            w_s[e, n] = w
            u_s[e, n] = u

    def scan_stages(it, states):
        r = range(heads)
        loaded = []
        for g in range(group):
            n = it * group + g
            loaded.append(([u_s[e, n] for e in r], [w_s[e, n] for e in r], [qd_s[e, n] for e in r],
                           [qkd_s[e, n] for e in r], [kdt_s[e, n] for e in r],
                           [egl_s[e, n][SUBLANES - 1:SUBLANES, :] for e in r]))
        yield
        outs = []
        for us, ws, qds, qkds, kdts, egls in loaded:
            v_news = [us[e] - _dot(ws[e], states[e]) for e in r]
            from_state = [_dot(qds[e], states[e]) for e in r]
            yield
            outs.append([from_state[e] + _dot(qkds[e], v_news[e]) for e in r])
            states = tuple(states[e] * egls[e] + _dot(kdts[e], v_news[e]) for e in r)
            yield
        return states, outs

    def store_outs(it, outs):
        for g, chunk_outs in enumerate(outs):
            for e in range(heads):
                o_s[e, it * group + g] = chunk_outs[e]

    def run(gen):
        return _interleave(gen, None)[0]

    n_groups = n_chunks // group
    store_prepped(run(prep_stages(0)))

    def body(it, states):
        prepped, (states, outs) = _interleave(prep_stages(it), scan_stages(it - 1, states))
        store_prepped(prepped)
        store_outs(it - 1, outs)
        return states

    states = lax.fori_loop(1, n_groups, body, tuple(jnp.zeros((hd, hd), jnp.float32) for _ in range(heads)))
    _, outs = run(scan_stages(n_groups - 1, states))
    store_outs(n_groups - 1, outs)

    for e in range(heads):
        cols = slice(e * hd, (e + 1) * hd)
        o = o_s[e].reshape(seq, hd)
        o = o * lax.rsqrt(jnp.mean(o * o, axis=-1, keepdims=True) + EPS) * gain_ref[...]
        o_ref[:, cols] = (o * z_ref[:, cols]).astype(o_ref.dtype)


def gated_deltanet(qkv, zact, gates, gain, batch, seq, n_heads):
    heads = DN_HEADS_PER_STEP
    c = DN_CHUNK
    n_chunks = seq // c
    assert n_heads % heads == 0 and n_chunks % DN_CHUNKS_PER_ITER == 0
    wide = heads * HEAD_DIM
    blk = lambda off: pl.BlockSpec((seq, wide), lambda b, h: (b, off // heads + h))
    gate_spec = pl.BlockSpec((seq, HEAD_DIM), lambda b, h: (b, 0))
    per_chunk = lambda width, dtype: pltpu.VMEM((heads, n_chunks, c, width), dtype)
    return pl.pallas_call(
        functools.partial(_deltanet_kernel, seq=seq),
        grid=(batch, n_heads // heads),
        in_specs=[blk(0), blk(n_heads), blk(2 * n_heads), blk(0),
                  gate_spec, gate_spec,
                  pl.BlockSpec((1, HEAD_DIM), lambda b, h: (0, 0))],
        out_specs=pl.BlockSpec((seq, wide), lambda b, h: (b, h)),
        out_shape=jax.ShapeDtypeStruct((batch * seq, n_heads * HEAD_DIM), MXU_DTYPE),
        scratch_shapes=[
            per_chunk(HEAD_DIM, MXU_DTYPE),
            per_chunk(HEAD_DIM, MXU_DTYPE),
            per_chunk(HEAD_DIM, MXU_DTYPE),
            per_chunk(2 * HEAD_DIM, MXU_DTYPE),
            per_chunk(HEAD_DIM, MXU_DTYPE),
            per_chunk(c, MXU_DTYPE),
            per_chunk(HEAD_DIM, jnp.float32),
            pltpu.VMEM((heads, n_chunks, SUBLANES, HEAD_DIM), jnp.float32),
            per_chunk(HEAD_DIM, MXU_DTYPE),
            per_chunk(HEAD_DIM, jnp.float32),
            per_chunk(c, MXU_DTYPE),
            per_chunk(HEAD_DIM, jnp.float32),
        ],
        compiler_params=_cparams("parallel", "parallel"),
        name="gated_deltanet",
    )(qkv, qkv, qkv, zact, *gates, gain)


def kernel(x, ffn1_norm, ffn1_w_gate, ffn1_w_up, ffn1_w_down, mix_norm, w_in, conv_w, a_log, dt_bias,
           dn_norm, w_out, ffn2_norm, ffn2_w_gate, ffn2_w_up, ffn2_w_down, final_norm):
    batch, seq, d_model = x.shape
    n_heads = a_log.shape[0]
    d_head_group = n_heads * HEAD_DIM
    h = x.reshape(batch * seq, d_model)

    h, xn = ffn(h, ffn1_norm, ffn1_w_gate, ffn1_w_up, ffn1_w_down, mix_norm, True, MXU_DTYPE)

    wt = w_in.T
    qkv_attn = matmul_nt(xn, wt, 0, 3 * d_head_group, MXU_DTYPE, lead_cols=d_head_group, lead_scale=ATTN_Q_SCALE)
    qkv_dn = proj_conv_qkv(xn, wt, conv_w, 3 * d_head_group, seq)
    zact = matmul_nt(xn, wt, 6 * d_head_group, d_head_group, jnp.float32, act=_silu)
    n_main = 7 * d_head_group
    rows_pad = lambda t: jnp.pad(t.astype(jnp.float32), ((0, HEAD_DIM - n_heads), (0, 0)))
    lane_pad = lambda t: jnp.pad(t.astype(jnp.float32).reshape(1, n_heads), ((0, 0), (0, HEAD_DIM - n_heads)))
    gates = dn_gates(xn, rows_pad(wt[n_main:n_main + n_heads]), rows_pad(wt[n_main + n_heads:]),
                     lane_pad(a_log), lane_pad(dt_bias))
    attn = dilated_attention(qkv_attn, batch, seq, n_heads)
    dn = gated_deltanet(qkv_dn, zact, gates, dn_norm.astype(jnp.float32).reshape(1, HEAD_DIM),
                        batch, seq, n_heads)
    h = matmul2_residual(attn, dn, w_out, h)

    out = ffn(h, ffn2_norm, ffn2_w_gate, ffn2_w_up, ffn2_w_down, final_norm, False, jnp.float32)
    return out.reshape(batch, seq, d_model)
```

```python
import functools
import math

import jax
import jax.numpy as jnp
from jax import lax
from jax.experimental import pallas as pl
from jax.experimental.pallas import tpu as pltpu

EPS = 1e-6
HEAD_DIM = 128
DILATED_CONFIGS = ((128, 1), (512, 4), (2048, 16))
CONV_WIDTH = 4
DN_CHUNK = 128
DN_HEADS_PER_STEP = 2
DN_CHUNKS_PER_ITER = 8
MASKED_BIAS = -1e30
ATTN_Q_SCALE = HEAD_DIM ** -0.5 * math.log2(math.e)
SUBLANES = 8

VMEM_LIMIT_BYTES = 56 * 1024 * 1024
FFN_VMEM_LIMIT_BYTES = 60 * 1024 * 1024
MXU_DTYPE = jnp.bfloat16


def _cparams(*sem):
    return pltpu.CompilerParams(dimension_semantics=sem, vmem_limit_bytes=VMEM_LIMIT_BYTES)


def _dot(a, b):
    return jnp.dot(a.astype(MXU_DTYPE), b.astype(MXU_DTYPE), preferred_element_type=jnp.float32)


def _dot_nt(a, b):
    return lax.dot_general(a.astype(MXU_DTYPE), b.astype(MXU_DTYPE), (((1,), (1,)), ((), ())),
                           preferred_element_type=jnp.float32)


def _silu(x):
    hx = 0.5 * x
    return hx + hx * jnp.tanh(hx)


PROJ_CONV_ROW_CHUNK = 512
FFN_ROW_CHUNK = 32
FFN_NORM_PIECES = 4
FFN_IN_CHUNKS = 8
FFN_K_CHUNK = 1024
FFN_N_CHUNK = 512


def _rmsnorm_rows(src_ref, w_ref, dst_ref, row0, n_rows):
    w = w_ref[...]
    step_rows = FFN_ROW_CHUNK * FFN_NORM_PIECES

    def body(r, carry):
        pieces = [pl.ds(pl.multiple_of(row0 + r * step_rows + p * FFN_ROW_CHUNK, FFN_ROW_CHUNK), FFN_ROW_CHUNK)
                  for p in range(FFN_NORM_PIECES)]
        sq = [src_ref[rows, :] for rows in pieces]
        inv = [lax.rsqrt(jnp.mean(x * x, axis=-1, keepdims=True) + EPS) for x in sq]
        for rows, scale in zip(pieces, inv):
            dst_ref[rows, :] = ((src_ref[rows, :] * scale) * w).astype(dst_ref.dtype)
        return carry
    lax.fori_loop(0, n_rows // step_rows, body, 0)


def _ffn_kernel(h_hbm, nw_ref, wg_ref, wu_ref, wd_ref, onw_ref, *rest, emit_res):
    if emit_res:
        res_hbm, norm_hbm, acc, xn_s, sem_in, sem_out = rest
    else:
        norm_hbm, acc, xn_s, sem_in, sem_out = rest
    i, j = pl.program_id(0), pl.program_id(1)
    n_i, n_j = pl.num_programs(0), pl.num_programs(1)
    bm, d = acc.shape
    rows_in = bm // FFN_IN_CHUNKS
    tile = lambda ref, t: ref.at[pl.ds(pl.multiple_of(t * bm, bm), bm), :]

    def in_copy(t, c):
        part = pl.ds(pl.multiple_of(c * rows_in, rows_in), rows_in)
        return pltpu.make_async_copy(tile(h_hbm, t).at[part, :], acc.at[part, :], sem_in.at[c])

    res_copy = lambda t: pltpu.make_async_copy(acc, tile(res_hbm, t), sem_out.at[0])
    norm_copy = lambda t: pltpu.make_async_copy(xn_s if emit_res else acc, tile(norm_hbm, t), sem_out.at[1])

    @pl.when(j == 0)
    def _():
        @pl.when(i > 0)
        def _():
            (res_copy if emit_res else norm_copy)(i - 1).wait()

        for c in range(FFN_IN_CHUNKS):
            in_copy(i, c).start()
        if emit_res:
            @pl.when(i > 0)
            def _():
                norm_copy(i - 1).wait()

        def land(c, carry):
            in_copy(i, c).wait()
            _rmsnorm_rows(acc, nw_ref, xn_s, c * rows_in, rows_in)
            return carry
        lax.fori_loop(0, FFN_IN_CHUNKS, land, 0)

    g = u = None
    for kc in range(d // FFN_K_CHUNK):
        ks = slice(kc * FFN_K_CHUNK, (kc + 1) * FFN_K_CHUNK)
        xk = xn_s[:, ks]
        gk, uk = _dot(xk, wg_ref[ks, :]), _dot(xk, wu_ref[ks, :])
        g, u = (gk, uk) if g is None else (g + gk, u + uk)
    act = ((0.5 * _silu(g)) * u).astype(MXU_DTYPE)
    for nc in range(d // FFN_N_CHUNK):
        ns = slice(nc * FFN_N_CHUNK, (nc + 1) * FFN_N_CHUNK)
        acc[:, ns] += _dot(act, wd_ref[:, ns])

    @pl.when(j == n_j - 1)
    def _():
        if emit_res:
            res_copy(i).start()
            _rmsnorm_rows(acc, onw_ref, xn_s, 0, bm)
        else:
            _rmsnorm_rows(acc, onw_ref, acc, 0, bm)
        norm_copy(i).start()

        @pl.when(i == n_i - 1)
        def _():
            if emit_res:
                res_copy(i).wait()
            norm_copy(i).wait()


def ffn(h, norm_w, w_gate, w_up, w_down, out_norm_w, emit_res, norm_dtype, bm=1024, bf=256):
    t, d = h.shape
    f = w_gate.shape[1]
    assert t % bm == 0 and f % bf == 0 and d % FFN_K_CHUNK == 0 and d % FFN_N_CHUNK == 0
    assert bm % (FFN_IN_CHUNKS * FFN_ROW_CHUNK * FFN_NORM_PIECES) == 0
    assert emit_res or norm_dtype == jnp.float32
    any_spec = pl.BlockSpec(memory_space=pl.ANY)
    row_spec = pl.BlockSpec((1, d), lambda i, j: (0, 0))
    norm_shape = jax.ShapeDtypeStruct((t, d), norm_dtype)
    return pl.pallas_call(
        functools.partial(_ffn_kernel, emit_res=emit_res),
        grid=(t // bm, f // bf),
        in_specs=[any_spec, row_spec,
                  pl.BlockSpec((d, bf), lambda i, j: (0, j)),
                  pl.BlockSpec((d, bf), lambda i, j: (0, j)),
                  pl.BlockSpec((bf, d), lambda i, j: (j, 0)),
                  row_spec],
        out_specs=[any_spec, any_spec] if emit_res else any_spec,
        out_shape=[jax.ShapeDtypeStruct((t, d), jnp.float32), norm_shape] if emit_res else norm_shape,
        scratch_shapes=[pltpu.VMEM((bm, d), jnp.float32), pltpu.VMEM((bm, d), MXU_DTYPE),
                        pltpu.SemaphoreType.DMA((FFN_IN_CHUNKS,)), pltpu.SemaphoreType.DMA((2,))],
        compiler_params=pltpu.CompilerParams(dimension_semantics=("arbitrary", "arbitrary"),
                                             vmem_limit_bytes=FFN_VMEM_LIMIT_BYTES),
        name="ffn",
    )(h, norm_w.reshape(1, d), w_gate, w_up, w_down, out_norm_w.reshape(1, d))


def _mm_kernel(a_ref, w_ref, o_ref, *, act, lead_blocks, lead_scale):
    acc = _dot_nt(a_ref[...], w_ref[...])
    if lead_blocks:
        acc = acc * jnp.where(pl.program_id(1) < lead_blocks, lead_scale, 1.0)
    o_ref[...] = (act(acc) if act else acc).astype(o_ref.dtype)


def matmul_nt(a, wt, row0, n, out_dtype, act=None, lead_cols=0, lead_scale=1.0, bm=1024, bn=512):
    t, k = a.shape
    assert row0 % bn == 0 and n % bn == 0 and lead_cols % bn == 0
    return pl.pallas_call(
        functools.partial(_mm_kernel, act=act, lead_blocks=lead_cols // bn, lead_scale=lead_scale),
        grid=(t // bm, n // bn),
        in_specs=[pl.BlockSpec((bm, k), lambda i, j: (i, 0)),
                  pl.BlockSpec((bn, k), lambda i, j: (row0 // bn + j, 0))],
        out_specs=pl.BlockSpec((bm, bn), lambda i, j: (i, j)),
        out_shape=jax.ShapeDtypeStruct((t, n), out_dtype),
        compiler_params=_cparams("parallel", "parallel"),
        name="matmul_nt",
    )(a, wt)


def _proj_conv_kernel(a_ref, w_ref, cw_ref, o_ref, acc_s, *, q_blocks, k_blocks):
    seq, bn = o_ref.shape
    chunk = min(PROJ_CONV_ROW_CHUNK, seq)
    w = w_ref[...].astype(MXU_DTYPE)
    cw = cw_ref[...]
    j = pl.program_id(1)
    is_v = j >= q_blocks + k_blocks
    qk_scale = jnp.where(j < q_blocks, HEAD_DIM ** -0.5, 1.0)
    row = lax.broadcasted_iota(jnp.int32, (SUBLANES, bn), 0)
    acc_s[:SUBLANES, :] = jnp.zeros((SUBLANES, bn), jnp.float32)
    for r in range(seq // chunk):
        acc_s[SUBLANES + r * chunk:SUBLANES + (r + 1) * chunk, :] = _dot_nt(a_ref[r * chunk:(r + 1) * chunk, :], w)
    for r in range(seq // chunk):
        y = None
        for i in range(CONV_WIDTH):
            start = SUBLANES + r * chunk - (CONV_WIDTH - 1 - i)
            term = cw[i:i + 1, :] * acc_s[start:start + chunk, :]
            y = term if y is None else y + term
        y = _silu(y)
        heads = []
        for g in range(bn // HEAD_DIM):
            yg = y[:, g * HEAD_DIM:(g + 1) * HEAD_DIM]
            inv = lax.rsqrt(jnp.sum(yg * yg, axis=-1, keepdims=True) + EPS) * qk_scale
            heads.append(yg * jnp.where(is_v, 1.0, inv))
        o_ref[r * chunk:(r + 1) * chunk, :] = jnp.concatenate(heads, axis=1)


def proj_conv_qkv(a, wt, conv_w, row0, seq, bn=256):
    t, k = a.shape
    n = conv_w.shape[1]
    width = n // 3
    assert row0 % bn == 0 and width % bn == 0 and bn % HEAD_DIM == 0
    return pl.pallas_call(
        functools.partial(_proj_conv_kernel, q_blocks=width // bn, k_blocks=width // bn),
        grid=(t // seq, n // bn),
        in_specs=[pl.BlockSpec((seq, k), lambda i, j: (i, 0)),
                  pl.BlockSpec((bn, k), lambda i, j: (row0 // bn + j, 0)),
                  pl.BlockSpec((CONV_WIDTH, bn), lambda i, j: (0, j))],
        out_specs=pl.BlockSpec((seq, bn), lambda i, j: (i, j)),
        out_shape=jax.ShapeDtypeStruct((t, n), jnp.float32),
        scratch_shapes=[pltpu.VMEM((SUBLANES + seq, bn), jnp.float32)],
        compiler_params=_cparams("parallel", "parallel"),
        name="proj_conv_qkv",
    )(a, wt, conv_w)


def _mm2_res_kernel(a1_ref, a2_ref, w1_ref, w2_ref, r_ref, o_ref):
    o_ref[...] = r_ref[...] + (_dot(a1_ref[...], w1_ref[...]) + _dot(a2_ref[...], w2_ref[...]))


def matmul2_residual(a1, a2, w, res, bm=512, bn=1024):
    t, k1 = a1.shape
    assert a2.shape[1] == k1 and w.shape[0] == 2 * k1
    n = w.shape[1]
    return pl.pallas_call(
        _mm2_res_kernel,
        grid=(n // bn, t // bm),
        in_specs=[pl.BlockSpec((bm, k1), lambda j, i: (i, 0)),
                  pl.BlockSpec((bm, k1), lambda j, i: (i, 0)),
                  pl.BlockSpec((k1, bn), lambda j, i: (0, j)),
                  pl.BlockSpec((k1, bn), lambda j, i: (1, j)),
                  pl.BlockSpec((bm, bn), lambda j, i: (i, j))],
        out_specs=pl.BlockSpec((bm, bn), lambda j, i: (i, j)),
        out_shape=jax.ShapeDtypeStruct((t, n), jnp.float32),
        compiler_params=_cparams("parallel", "parallel"),
        name="matmul2_residual",
    )(a1, a2, w, w, res)


def _attn_kernel(q_ref, k_ref, v_ref, o_ref, bias_ref, *, seq, tq):
    @pl.when((pl.program_id(0) == 0) & (pl.program_id(1) == 0))
    def _():
        i = lax.broadcasted_iota(jnp.int32, (tq, seq), 0)
        jj = lax.broadcasted_iota(jnp.int32, (tq, seq), 1)
        dist = i - (jj - (seq - tq))
        count = jnp.zeros((tq, seq), jnp.int32)
        for window, d in DILATED_CONFIGS:
            hit = (dist >= 0) & (dist <= (window // d) * d) & ((dist & (d - 1)) == 0)
            count = count + hit.astype(jnp.int32)
        bias_ref[...] = jnp.where(
            count == 3, math.log2(3.0),
            jnp.where(count == 2, 1.0, jnp.where(count == 1, 0.0, MASKED_BIAS)))

    k = k_ref[...].astype(MXU_DTYPE)
    v = v_ref[...].astype(MXU_DTYPE)
    n_tiles = seq // tq

    def scores(qi):
        kend = (qi + 1) * tq
        return _dot_nt(q_ref[qi * tq:kend, :], k[:kend]) + bias_ref[:, seq - kend:]

    s = scores(0)
    for qi in range(n_tiles):
        kend = (qi + 1) * tq
        s_next = scores(qi + 1) if qi + 1 < n_tiles else None
        m = jnp.max(s, axis=-1, keepdims=True)
        p = jnp.exp2(s - m)
        den = jnp.sum(p, axis=-1, keepdims=True)
        o = _dot(p, v[:kend])
        o_ref[qi * tq:kend, :] = (o / den).astype(o_ref.dtype)
        s = s_next


def dilated_attention(proj, batch, seq, n_heads, tq=256):
    tq = min(tq, seq)
    for window, d in DILATED_CONFIGS:
        assert d & (d - 1) == 0
    blk = lambda off: pl.BlockSpec((seq, HEAD_DIM), lambda b, h: (b, off + h))
    return pl.pallas_call(
        functools.partial(_attn_kernel, seq=seq, tq=tq),
        grid=(batch, n_heads),
        in_specs=[blk(0), blk(n_heads), blk(2 * n_heads)],
        out_specs=pl.BlockSpec((seq, HEAD_DIM), lambda b, h: (b, h)),
        out_shape=jax.ShapeDtypeStruct((batch * seq, n_heads * HEAD_DIM), MXU_DTYPE),
        scratch_shapes=[pltpu.VMEM((tq, seq), jnp.float32)],
        compiler_params=_cparams("arbitrary", "arbitrary"),
        name="dilated_attention",
    )(proj, proj, proj)


def _dn_gates_kernel(x_ref, wb_ref, wa_ref, alog_ref, dtb_ref, beta_ref, gc_ref):
    c = DN_CHUNK
    x = x_ref[...]
    beta_ref[...] = jax.nn.sigmoid(_dot_nt(x, wb_ref[...]))
    z = _dot_nt(x, wa_ref[...]) + dtb_ref[...]
    softplus = jnp.maximum(z, 0.0) + jnp.log1p(jnp.exp(-jnp.abs(z)))
    gc = -jnp.exp(alog_ref[...]) * softplus
    row = lax.broadcasted_iota(jnp.int32, gc.shape, 0)
    s = 1
    while s < c:
        gc = gc + jnp.where((row & (c - 1)) >= s, pltpu.roll(gc, s, axis=0), 0.0)
        s *= 2
    gc_ref[...] = gc


def dn_gates(xn, wt_beta, wt_decay, a_row, dt_row, bm=1024):
    t, d = xn.shape
    out = jax.ShapeDtypeStruct((t, HEAD_DIM), jnp.float32)
    tok = pl.BlockSpec((bm, HEAD_DIM), lambda i: (i, 0))
    wspec = pl.BlockSpec((HEAD_DIM, d), lambda i: (0, 0))
    row_spec = pl.BlockSpec((1, HEAD_DIM), lambda i: (0, 0))
    return pl.pallas_call(
        _dn_gates_kernel,
        grid=(t // bm,),
        in_specs=[pl.BlockSpec((bm, d), lambda i: (i, 0)), wspec, wspec, row_spec, row_spec],
        out_specs=[tok, tok],
        out_shape=[out, out],
        compiler_params=_cparams("parallel"),
        name="dn_gates",
    )(xn, wt_beta, wt_decay, a_row, dt_row)


def _lane_column(x, lane, col):
    picked = jnp.sum(jnp.where(lane == col, x, 0.0), axis=-1, keepdims=True)
    return jnp.broadcast_to(picked, x.shape)


def _unit_lower_inverses(nmats, ci, cj):
    c = nmats[0].shape[0]
    eye = (ci == cj).astype(jnp.float32)
    same8 = (ci >> 3) == (cj >> 3)
    ms = [-jnp.where(same8, nmat, 0.0) for nmat in nmats]
    tinvs = [eye + m for m in ms]
    ms = [_dot(m, m) for m in ms]
    yield
    xs = [_dot(m, jnp.concatenate([m, tinv], axis=1)) for m, tinv in zip(ms, tinvs)]
    yield
    tinvs = [tinv + x[:, c:] for tinv, x in zip(tinvs, xs)]
    ys = [_dot(x[:, :c], tinv) for x, tinv in zip(xs, tinvs)]
    yield
    tinvs = [tinv + y for tinv, y in zip(tinvs, ys)]
    shift = 3
    while (1 << shift) < c:
        pick = ((ci >> (shift + 1)) == (cj >> (shift + 1))) & ((ci >> shift) != (cj >> shift))
        ys = [_dot(jnp.where(pick, nmat, 0.0), tinv) for nmat, tinv in zip(nmats, tinvs)]
        yield
        ys = [_dot(tinv, y) for tinv, y in zip(tinvs, ys)]
        yield
        tinvs = [tinv - y for tinv, y in zip(tinvs, ys)]
        shift += 1
    return tinvs


def _interleave(first, second):
    results = [None, None]
    live = [first, second]
    while any(g is not None for g in live):
        for idx, g in enumerate(live):
            if g is None:
                continue
            try:
                next(g)
            except StopIteration as stop:
                results[idx] = stop.value
                live[idx] = None
    return results


def _deltanet_kernel(q_ref, k_ref, v_ref, z_ref, beta_ref, gc_ref, gain_ref, o_ref,
                     q_s, k_s, kb_s, rhs_s, qd_s, kdt_s, gc_s, egl_s, w_s, u_s, qkd_s, o_s, *, seq):
    c = DN_CHUNK
    hd = HEAD_DIM
    n_chunks = seq // c
    group = DN_CHUNKS_PER_ITER
    heads = DN_HEADS_PER_STEP
    lane = lax.broadcasted_iota(jnp.int32, (seq, hd), 1)

    for e in range(heads):
        cols = slice(e * hd, (e + 1) * hd)
        head = pl.program_id(1) * heads + e
        beta = _lane_column(beta_ref[...], lane, head)
        gc = _lane_column(gc_ref[...], lane, head).reshape(n_chunks, c, hd)
        eg = jnp.exp(gc)
        kd = jnp.exp(gc[:, c - 1:c, :] - gc)
        q = q_ref[:, cols]
        k = k_ref[:, cols]
        kb = k * beta
        kdt = (k * kd.reshape(seq, hd)).T
        eg2 = eg.reshape(seq, hd)
        q_s[e] = q.astype(q_s.dtype).reshape(n_chunks, c, hd)
        k_s[e] = k.astype(k_s.dtype).reshape(n_chunks, c, hd)
        kb_s[e] = kb.astype(kb_s.dtype).reshape(n_chunks, c, hd)
        rhs_s[e] = jnp.concatenate([kb * eg2, v_ref[:, cols] * beta], axis=1
                                   ).astype(rhs_s.dtype).reshape(n_chunks, c, 2 * hd)
        qd_s[e] = (q * eg2).astype(qd_s.dtype).reshape(n_chunks, c, hd)
        gc_s[e] = gc
        egl_s[e] = eg[:, c - SUBLANES:, :]
        for n in range(n_chunks):
            kdt_s[e, n] = kdt[:, n * c:(n + 1) * c].astype(kdt_s.dtype)

    ci = lax.broadcasted_iota(jnp.int32, (c, c), 0)
    cj = lax.broadcasted_iota(jnp.int32, (c, c), 1)

    def prep_stages(it):
        chains = [(e, it * group + g) for g in range(group) for e in range(heads)]
        ks = [k_s[e, n] for e, n in chains]
        kbs = [kb_s[e, n] for e, n in chains]
        qs = [q_s[e, n] for e, n in chains]
        gcs = [gc_s[e, n] for e, n in chains]
        rhss = [rhs_s[e, n] for e, n in chains]
        yield
        kks = [_dot_nt(kb, k) for kb, k in zip(kbs, ks)]
        qks = [_dot_nt(q, k) for q, k in zip(qs, ks)]
        yield
        decays = [jnp.exp(jnp.where(ci >= cj, gc - gc.T, -jnp.inf)) for gc in gcs]
        nmats = [jnp.where(ci > cj, kk * decay, 0.0) for kk, decay in zip(kks, decays)]
        tinvs = yield from _unit_lower_inverses(nmats, ci, cj)
        sols = [_dot(tinv, rhs) for tinv, rhs in zip(tinvs, rhss)]
        yield
        return [(e, n, (qk * decay).astype(qkd_s.dtype), sol[:, :hd].astype(w_s.dtype), sol[:, hd:])
                for (e, n), qk, decay, sol in zip(chains, qks, decays, sols)]

    def store_prepped(prepped):
        for e, n, qkd, w, u in prepped:
            qkd_s[e, n] = qkd
            w_s[e, n] = w
            u_s[e, n] = u

    def scan_stages(it, states):
        r = range(heads)
        loaded = []
        for g in range(group):
            n = it * group + g
            loaded.append(([u_s[e, n] for e in r], [w_s[e, n] for e in r], [qd_s[e, n] for e in r],
                           [qkd_s[e, n] for e in r], [kdt_s[e, n] for e in r],
                           [egl_s[e, n][SUBLANES - 1:SUBLANES, :] for e in r]))
        yield
        outs = []
        for us, ws, qds, qkds, kdts, egls in loaded:
            v_news = [us[e] - _dot(ws[e], states[e]) for e in r]
            from_state = [_dot(qds[e], states[e]) for e in r]
            yield
            outs.append([from_state[e] + _dot(qkds[e], v_news[e]) for e in r])
            states = tuple(states[e] * egls[e] + _dot(kdts[e], v_news[e]) for e in r)
            yield
        return states, outs

    def store_outs(it, outs):
        for g, chunk_outs in enumerate(outs):
            for e in range(heads):
                o_s[e, it * group + g] = chunk_outs[e]

    def run(gen):
        return _interleave(gen, None)[0]

    n_groups = n_chunks // group
    store_prepped(run(prep_stages(0)))

    def body(it, states):
        prepped, (states, outs) = _interleave(prep_stages(it), scan_stages(it - 1, states))
        store_prepped(prepped)
        store_outs(it - 1, outs)
        return states

    states = lax.fori_loop(1, n_groups, body, tuple(jnp.zeros((hd, hd), jnp.float32) for _ in range(heads)))
    _, outs = run(scan_stages(n_groups - 1, states))
    store_outs(n_groups - 1, outs)

    for e in range(heads):
        cols = slice(e * hd, (e + 1) * hd)
        o = o_s[e].reshape(seq, hd)
        o = o * lax.rsqrt(jnp.mean(o * o, axis=-1, keepdims=True) + EPS) * gain_ref[...]
        o_ref[:, cols] = (o * z_ref[:, cols]).astype(o_ref.dtype)


def gated_deltanet(qkv, zact, gates, gain, batch, seq, n_heads):
    heads = DN_HEADS_PER_STEP
    c = DN_CHUNK
    n_chunks = seq // c
    assert n_heads % heads == 0 and n_chunks % DN_CHUNKS_PER_ITER == 0
    wide = heads * HEAD_DIM
    blk = lambda off: pl.BlockSpec((seq, wide), lambda b, h: (b, off // heads + h))
    gate_spec = pl.BlockSpec((seq, HEAD_DIM), lambda b, h: (b, 0))
    per_chunk = lambda width, dtype: pltpu.VMEM((heads, n_chunks, c, width), dtype)
    return pl.pallas_call(
        functools.partial(_deltanet_kernel, seq=seq),
        grid=(batch, n_heads // heads),
        in_specs=[blk(0), blk(n_heads), blk(2 * n_heads), blk(0),
                  gate_spec, gate_spec,
                  pl.BlockSpec((1, HEAD_DIM), lambda b, h: (0, 0))],
        out_specs=pl.BlockSpec((seq, wide), lambda b, h: (b, h)),
        out_shape=jax.ShapeDtypeStruct((batch * seq, n_heads * HEAD_DIM), MXU_DTYPE),
        scratch_shapes=[
            per_chunk(HEAD_DIM, MXU_DTYPE),
            per_chunk(HEAD_DIM, MXU_DTYPE),
            per_chunk(HEAD_DIM, MXU_DTYPE),
            per_chunk(2 * HEAD_DIM, MXU_DTYPE),
            per_chunk(HEAD_DIM, MXU_DTYPE),
            per_chunk(c, MXU_DTYPE),
            per_chunk(HEAD_DIM, jnp.float32),
            pltpu.VMEM((heads, n_chunks, SUBLANES, HEAD_DIM), jnp.float32),
            per_chunk(HEAD_DIM, MXU_DTYPE),
            per_chunk(HEAD_DIM, jnp.float32),
            per_chunk(c, MXU_DTYPE),
            per_chunk(HEAD_DIM, jnp.float32),
        ],
        compiler_params=_cparams("parallel", "parallel"),
        name="gated_deltanet",
    )(qkv, qkv, qkv, zact, *gates, gain)


def kernel(x, ffn1_norm, ffn1_w_gate, ffn1_w_up, ffn1_w_down, mix_norm, w_in, conv_w, a_log, dt_bias,
           dn_norm, w_out, ffn2_norm, ffn2_w_gate, ffn2_w_up, ffn2_w_down, final_norm):
    batch, seq, d_model = x.shape
    n_heads = a_log.shape[0]
    d_head_group = n_heads * HEAD_DIM
    h = x.reshape(batch * seq, d_model)

    h, xn = ffn(h, ffn1_norm, ffn1_w_gate, ffn1_w_up, ffn1_w_down, mix_norm, True, MXU_DTYPE)

    wt = w_in.T
    qkv_attn = matmul_nt(xn, wt, 0, 3 * d_head_group, MXU_DTYPE, lead_cols=d_head_group, lead_scale=ATTN_Q_SCALE)
    qkv_dn = proj_conv_qkv(xn, wt, conv_w, 3 * d_head_group, seq)
    zact = matmul_nt(xn, wt, 6 * d_head_group, d_head_group, jnp.float32, act=_silu)
    n_main = 7 * d_head_group
    rows_pad = lambda t: jnp.pad(t.astype(jnp.float32), ((0, HEAD_DIM - n_heads), (0, 0)))
    lane_pad = lambda t: jnp.pad(t.astype(jnp.float32).reshape(1, n_heads), ((0, 0), (0, HEAD_DIM - n_heads)))
    gates = dn_gates(xn, rows_pad(wt[n_main:n_main + n_heads]), rows_pad(wt[n_main + n_heads:]),
                     lane_pad(a_log), lane_pad(dt_bias))
    attn = dilated_attention(qkv_attn, batch, seq, n_heads)
    dn = gated_deltanet(qkv_dn, zact, gates, dn_norm.astype(jnp.float32).reshape(1, HEAD_DIM),
                        batch, seq, n_heads)
    h = matmul2_residual(attn, dn, w_out, h)

    out = ffn(h, ffn2_norm, ffn2_w_gate, ffn2_w_up, ffn2_w_down, final_norm, False, jnp.float32)
    return out.reshape(batch, seq, d_model)
```

```python
import functools
import math

import jax
import jax.numpy as jnp
from jax import lax
from jax.experimental import pallas as pl
from jax.experimental.pallas import tpu as pltpu

EPS = 1e-6
HEAD_DIM = 128
DILATED_CONFIGS = ((128, 1), (512, 4), (2048, 16))
CONV_WIDTH = 4
DN_CHUNK = 128
DN_HEADS_PER_STEP = 2
DN_CHUNKS_PER_ITER = 8
MASKED_BIAS = -1e30
ATTN_Q_SCALE = HEAD_DIM ** -0.5 * math.log2(math.e)
SUBLANES = 8

VMEM_LIMIT_BYTES = 56 * 1024 * 1024
FFN_VMEM_LIMIT_BYTES = 60 * 1024 * 1024
MXU_DTYPE = jnp.bfloat16


def _cparams(*sem):
    return pltpu.CompilerParams(dimension_semantics=sem, vmem_limit_bytes=VMEM_LIMIT_BYTES)


def _dot(a, b):
    return jnp.dot(a.astype(MXU_DTYPE), b.astype(MXU_DTYPE), preferred_element_type=jnp.float32)


def _dot_nt(a, b):
    return lax.dot_general(a.astype(MXU_DTYPE), b.astype(MXU_DTYPE), (((1,), (1,)), ((), ())),
                           preferred_element_type=jnp.float32)


def _silu(x):
    hx = 0.5 * x
    return hx + hx * jnp.tanh(hx)


PROJ_CONV_ROW_CHUNK = 512
FFN_ROW_CHUNK = 32
FFN_NORM_PIECES = 4
FFN_IN_CHUNKS = 8
FFN_K_CHUNK = 1024
FFN_N_CHUNK = 512


def _rmsnorm_rows(src_ref, w_ref, dst_ref, row0, n_rows):
    w = w_ref[...]
    step_rows = FFN_ROW_CHUNK * FFN_NORM_PIECES

    def body(r, carry):
        pieces = [pl.ds(pl.multiple_of(row0 + r * step_rows + p * FFN_ROW_CHUNK, FFN_ROW_CHUNK), FFN_ROW_CHUNK)
                  for p in range(FFN_NORM_PIECES)]
        sq = [src_ref[rows, :] for rows in pieces]
        inv = [lax.rsqrt(jnp.mean(x * x, axis=-1, keepdims=True) + EPS) for x in sq]
        for rows, scale in zip(pieces, inv):
            dst_ref[rows, :] = ((src_ref[rows, :] * scale) * w).astype(dst_ref.dtype)
        return carry
    lax.fori_loop(0, n_rows // step_rows, body, 0)


def _ffn_kernel(h_hbm, nw_ref, wg_ref, wu_ref, wd_ref, onw_ref, *rest, emit_res):
    if emit_res:
        res_hbm, norm_hbm, acc, xn_s, sem_in, sem_out = rest
    else:
        norm_hbm, acc, xn_s, sem_in, sem_out = rest
    i, j = pl.program_id(0), pl.program_id(1)
    n_i, n_j = pl.num_programs(0), pl.num_programs(1)
    bm, d = acc.shape
    rows_in = bm // FFN_IN_CHUNKS
    tile = lambda ref, t: ref.at[pl.ds(pl.multiple_of(t * bm, bm), bm), :]

    def in_copy(t, c):
        part = pl.ds(pl.multiple_of(c * rows_in, rows_in), rows_in)
        return pltpu.make_async_copy(tile(h_hbm, t).at[part, :], acc.at[part, :], sem_in.at[c])

    res_copy = lambda t: pltpu.make_async_copy(acc, tile(res_hbm, t), sem_out.at[0])
    norm_copy = lambda t: pltpu.make_async_copy(xn_s if emit_res else acc, tile(norm_hbm, t), sem_out.at[1])

    @pl.when(j == 0)
    def _():
        @pl.when(i > 0)
        def _():
            (res_copy if emit_res else norm_copy)(i - 1).wait()

        for c in range(FFN_IN_CHUNKS):
            in_copy(i, c).start()
        if emit_res:
            @pl.when(i > 0)
            def _():
                norm_copy(i - 1).wait()

        def land(c, carry):
            in_copy(i, c).wait()
            _rmsnorm_rows(acc, nw_ref, xn_s, c * rows_in, rows_in)
            return carry
        lax.fori_loop(0, FFN_IN_CHUNKS, land, 0)

    g = u = None
    for kc in range(d // FFN_K_CHUNK):
        ks = slice(kc * FFN_K_CHUNK, (kc + 1) * FFN_K_CHUNK)
        xk = xn_s[:, ks]
        gk, uk = _dot(xk, wg_ref[ks, :]), _dot(xk, wu_ref[ks, :])
        g, u = (gk, uk) if g is None else (g + gk, u + uk)
    act = ((0.5 * _silu(g)) * u).astype(MXU_DTYPE)
    for nc in range(d // FFN_N_CHUNK):
        ns = slice(nc * FFN_N_CHUNK, (nc + 1) * FFN_N_CHUNK)
        acc[:, ns] += _dot(act, wd_ref[:, ns])

    @pl.when(j == n_j - 1)
    def _():
        if emit_res:
            res_copy(i).start()
            _rmsnorm_rows(acc, onw_ref, xn_s, 0, bm)
        else:
            _rmsnorm_rows(acc, onw_ref, acc, 0, bm)
        norm_copy(i).start()

        @pl.when(i == n_i - 1)
        def _():
            if emit_res:
                res_copy(i).wait()
            norm_copy(i).wait()


def ffn(h, norm_w, w_gate, w_up, w_down, out_norm_w, emit_res, norm_dtype, bm=1024, bf=256):
    t, d = h.shape
    f = w_gate.shape[1]
    assert t % bm == 0 and f % bf == 0 and d % FFN_K_CHUNK == 0 and d % FFN_N_CHUNK == 0
    assert bm % (FFN_IN_CHUNKS * FFN_ROW_CHUNK * FFN_NORM_PIECES) == 0
    assert emit_res or norm_dtype == jnp.float32
    any_spec = pl.BlockSpec(memory_space=pl.ANY)
    row_spec = pl.BlockSpec((1, d), lambda i, j: (0, 0))
    norm_shape = jax.ShapeDtypeStruct((t, d), norm_dtype)
    return pl.pallas_call(
        functools.partial(_ffn_kernel, emit_res=emit_res),
        grid=(t // bm, f // bf),
        in_specs=[any_spec, row_spec,
                  pl.BlockSpec((d, bf), lambda i, j: (0, j)),
                  pl.BlockSpec((d, bf), lambda i, j: (0, j)),
                  pl.BlockSpec((bf, d), lambda i, j: (j, 0)),
                  row_spec],
        out_specs=[any_spec, any_spec] if emit_res else any_spec,
        out_shape=[jax.ShapeDtypeStruct((t, d), jnp.float32), norm_shape] if emit_res else norm_shape,
        scratch_shapes=[pltpu.VMEM((bm, d), jnp.float32), pltpu.VMEM((bm, d), MXU_DTYPE),
                        pltpu.SemaphoreType.DMA((FFN_IN_CHUNKS,)), pltpu.SemaphoreType.DMA((2,))],
        compiler_params=pltpu.CompilerParams(dimension_semantics=("arbitrary", "arbitrary"),
                                             vmem_limit_bytes=FFN_VMEM_LIMIT_BYTES),
        name="ffn",
    )(h, norm_w.reshape(1, d), w_gate, w_up, w_down, out_norm_w.reshape(1, d))


def _mm_kernel(a_ref, w_ref, o_ref, *, act, lead_blocks, lead_scale):
    acc = _dot_nt(a_ref[...], w_ref[...])
    if lead_blocks:
        acc = acc * jnp.where(pl.program_id(1) < lead_blocks, lead_scale, 1.0)
    o_ref[...] = (act(acc) if act else acc).astype(o_ref.dtype)


def matmul_nt(a, wt, row0, n, out_dtype, act=None, lead_cols=0, lead_scale=1.0, bm=1024, bn=512):
    t, k = a.shape
    assert row0 % bn == 0 and n % bn == 0 and lead_cols % bn == 0
    return pl.pallas_call(
        functools.partial(_mm_kernel, act=act, lead_blocks=lead_cols // bn, lead_scale=lead_scale),
        grid=(t // bm, n // bn),
        in_specs=[pl.BlockSpec((bm, k), lambda i, j: (i, 0)),
                  pl.BlockSpec((bn, k), lambda i, j: (row0 // bn + j, 0))],
        out_specs=pl.BlockSpec((bm, bn), lambda i, j: (i, j)),
        out_shape=jax.ShapeDtypeStruct((t, n), out_dtype),
        compiler_params=_cparams("parallel", "parallel"),
        name="matmul_nt",
    )(a, wt)


def _proj_conv_kernel(a_ref, w_ref, cw_ref, o_ref, acc_s, *, q_blocks, k_blocks):
    seq, bn = o_ref.shape
    chunk = min(PROJ_CONV_ROW_CHUNK, seq)
    w = w_ref[...].astype(MXU_DTYPE)
    cw = cw_ref[...]
    j = pl.program_id(1)
    is_v = j >= q_blocks + k_blocks
    qk_scale = jnp.where(j < q_blocks, HEAD_DIM ** -0.5, 1.0)
    row = lax.broadcasted_iota(jnp.int32, (SUBLANES, bn), 0)
    acc_s[:SUBLANES, :] = jnp.zeros((SUBLANES, bn), jnp.float32)
    for r in range(seq // chunk):
        acc_s[SUBLANES + r * chunk:SUBLANES + (r + 1) * chunk, :] = _dot_nt(a_ref[r * chunk:(r + 1) * chunk, :], w)
    for r in range(seq // chunk):
        y = None
        for i in range(CONV_WIDTH):
            start = SUBLANES + r * chunk - (CONV_WIDTH - 1 - i)
            term = cw[i:i + 1, :] * acc_s[start:start + chunk, :]
            y = term if y is None else y + term
        y = _silu(y)
        heads = []
        for g in range(bn // HEAD_DIM):
            yg = y[:, g * HEAD_DIM:(g + 1) * HEAD_DIM]
            inv = lax.rsqrt(jnp.sum(yg * yg, axis=-1, keepdims=True) + EPS) * qk_scale
            heads.append(yg * jnp.where(is_v, 1.0, inv))
        o_ref[r * chunk:(r + 1) * chunk, :] = jnp.concatenate(heads, axis=1)


def proj_conv_qkv(a, wt, conv_w, row0, seq, bn=256):
    t, k = a.shape
    n = conv_w.shape[1]
    width = n // 3
    assert row0 % bn == 0 and width % bn == 0 and bn % HEAD_DIM == 0
    return pl.pallas_call(
        functools.partial(_proj_conv_kernel, q_blocks=width // bn, k_blocks=width // bn),
        grid=(t // seq, n // bn),
        in_specs=[pl.BlockSpec((seq, k), lambda i, j: (i, 0)),
                  pl.BlockSpec((bn, k), lambda i, j: (row0 // bn + j, 0)),
                  pl.BlockSpec((CONV_WIDTH, bn), lambda i, j: (0, j))],
        out_specs=pl.BlockSpec((seq, bn), lambda i, j: (i, j)),
        out_shape=jax.ShapeDtypeStruct((t, n), jnp.float32),
        scratch_shapes=[pltpu.VMEM((SUBLANES + seq, bn), jnp.float32)],
        compiler_params=_cparams("parallel", "parallel"),
        name="proj_conv_qkv",
    )(a, wt, conv_w)


def _mm2_res_kernel(a1_ref, a2_ref, w1_ref, w2_ref, r_ref, o_ref):
    o_ref[...] = r_ref[...] + (_dot(a1_ref[...], w1_ref[...]) + _dot(a2_ref[...], w2_ref[...]))


def matmul2_residual(a1, a2, w, res, bm=512, bn=1024):
    t, k1 = a1.shape
    assert a2.shape[1] == k1 and w.shape[0] == 2 * k1
    n = w.shape[1]
    return pl.pallas_call(
        _mm2_res_kernel,
        grid=(n // bn, t // bm),
        in_specs=[pl.BlockSpec((bm, k1), lambda j, i: (i, 0)),
                  pl.BlockSpec((bm, k1), lambda j, i: (i, 0)),
                  pl.BlockSpec((k1, bn), lambda j, i: (0, j)),
                  pl.BlockSpec((k1, bn), lambda j, i: (1, j)),
                  pl.BlockSpec((bm, bn), lambda j, i: (i, j))],
        out_specs=pl.BlockSpec((bm, bn), lambda j, i: (i, j)),
        out_shape=jax.ShapeDtypeStruct((t, n), jnp.float32),
        compiler_params=_cparams("parallel", "parallel"),
        name="matmul2_residual",
    )(a1, a2, w, w, res)


def _attn_kernel(q_ref, k_ref, v_ref, o_ref, bias_ref, *, seq, tq):
    @pl.when((pl.program_id(0) == 0) & (pl.program_id(1) == 0))
    def _():
        i = lax.broadcasted_iota(jnp.int32, (tq, seq), 0)
        jj = lax.broadcasted_iota(jnp.int32, (tq, seq), 1)
        dist = i - (jj - (seq - tq))
        count = jnp.zeros((tq, seq), jnp.int32)
        for window, d in DILATED_CONFIGS:
            hit = (dist >= 0) & (dist <= (window // d) * d) & ((dist & (d - 1)) == 0)
            count = count + hit.astype(jnp.int32)
        bias_ref[...] = jnp.where(
            count == 3, math.log2(3.0),
            jnp.where(count == 2, 1.0, jnp.where(count == 1, 0.0, MASKED_BIAS)))

    k = k_ref[...].astype(MXU_DTYPE)
    v = v_ref[...].astype(MXU_DTYPE)
    n_tiles = seq // tq

    def scores(qi):
        kend = (qi + 1) * tq
        return _dot_nt(q_ref[qi * tq:kend, :], k[:kend]) + bias_ref[:, seq - kend:]

    s = scores(0)
    for qi in range(n_tiles):
        kend = (qi + 1) * tq
        s_next = scores(qi + 1) if qi + 1 < n_tiles else None
        m = jnp.max(s, axis=-1, keepdims=True)
        p = jnp.exp2(s - m)
        den = jnp.sum(p, axis=-1, keepdims=True)
        o = _dot(p, v[:kend])
        o_ref[qi * tq:kend, :] = (o / den).astype(o_ref.dtype)
        s = s_next


def dilated_attention(proj, batch, seq, n_heads, tq=256):
    tq = min(tq, seq)
    for window, d in DILATED_CONFIGS:
        assert d & (d - 1) == 0
    blk = lambda off: pl.BlockSpec((seq, HEAD_DIM), lambda b, h: (b, off + h))
    return pl.pallas_call(
        functools.partial(_attn_kernel, seq=seq, tq=tq),
        grid=(batch, n_heads),
        in_specs=[blk(0), blk(n_heads), blk(2 * n_heads)],
        out_specs=pl.BlockSpec((seq, HEAD_DIM), lambda b, h: (b, h)),
        out_shape=jax.ShapeDtypeStruct((batch * seq, n_heads * HEAD_DIM), MXU_DTYPE),
        scratch_shapes=[pltpu.VMEM((tq, seq), jnp.float32)],
        compiler_params=_cparams("arbitrary", "arbitrary"),
        name="dilated_attention",
    )(proj, proj, proj)


def _dn_gates_kernel(x_ref, wb_ref, wa_ref, alog_ref, dtb_ref, beta_ref, gc_ref):
    c = DN_CHUNK
    x = x_ref[...]
    beta_ref[...] = jax.nn.sigmoid(_dot_nt(x, wb_ref[...]))
    z = _dot_nt(x, wa_ref[...]) + dtb_ref[...]
    softplus = jnp.maximum(z, 0.0) + jnp.log1p(jnp.exp(-jnp.abs(z)))
    gc = -jnp.exp(alog_ref[...]) * softplus
    row = lax.broadcasted_iota(jnp.int32, gc.shape, 0)
    s = 1
    while s < c:
        gc = gc + jnp.where((row & (c - 1)) >= s, pltpu.roll(gc, s, axis=0), 0.0)
        s *= 2
    gc_ref[...] = gc


def dn_gates(xn, wt_beta, wt_decay, a_row, dt_row, bm=1024):
    t, d = xn.shape
    out = jax.ShapeDtypeStruct((t, HEAD_DIM), jnp.float32)
    tok = pl.BlockSpec((bm, HEAD_DIM), lambda i: (i, 0))
    wspec = pl.BlockSpec((HEAD_DIM, d), lambda i: (0, 0))
    row_spec = pl.BlockSpec((1, HEAD_DIM), lambda i: (0, 0))
    return pl.pallas_call(
        _dn_gates_kernel,
        grid=(t // bm,),
        in_specs=[pl.BlockSpec((bm, d), lambda i: (i, 0)), wspec, wspec, row_spec, row_spec],
        out_specs=[tok, tok],
        out_shape=[out, out],
        compiler_params=_cparams("parallel"),
        name="dn_gates",
    )(xn, wt_beta, wt_decay, a_row, dt_row)


def _lane_column(x, lane, col):
    picked = jnp.sum(jnp.where(lane == col, x, 0.0), axis=-1, keepdims=True)
    return jnp.broadcast_to(picked, x.shape)


def _unit_lower_inverses(nmats, ci, cj):
    c = nmats[0].shape[0]
    eye = (ci == cj).astype(jnp.float32)
    same8 = (ci >> 3) == (cj >> 3)
    ms = [-jnp.where(same8, nmat, 0.0) for nmat in nmats]
    tinvs = [eye + m for m in ms]
    ms = [_dot(m, m) for m in ms]
    yield
    xs = [_dot(m, jnp.concatenate([m, tinv], axis=1)) for m, tinv in zip(ms, tinvs)]
    yield
    tinvs = [tinv + x[:, c:] for tinv, x in zip(tinvs, xs)]
    ys = [_dot(x[:, :c], tinv) for x, tinv in zip(xs, tinvs)]
    yield
    tinvs = [tinv + y for tinv, y in zip(tinvs, ys)]
    shift = 3
    while (1 << shift) < c:
        pick = ((ci >> (shift + 1)) == (cj >> (shift + 1))) & ((ci >> shift) != (cj >> shift))
        ys = [_dot(jnp.where(pick, nmat, 0.0), tinv) for nmat, tinv in zip(nmats, tinvs)]
        yield
        ys = [_dot(tinv, y) for tinv, y in zip(tinvs, ys)]
        yield
        tinvs = [tinv - y for tinv, y in zip(tinvs, ys)]
        shift += 1
    return tinvs


def _interleave(first, second):
    results = [None, None]
    live = [first, second]
    while any(g is not None for g in live):
        for idx, g in enumerate(live):
            if g is None:
                continue
            try:
                next(g)
            except StopIteration as stop:
                results[idx] = stop.value
                live[idx] = None
    return results


def _deltanet_kernel(q_ref, k_ref, v_ref, z_ref, beta_ref, gc_ref, gain_ref, o_ref,
                     q_s, k_s, kb_s, rhs_s, qd_s, kdt_s, gc_s, egl_s, mc_s, b_s, d_s, o_s, *, seq):
    c = DN_CHUNK
    hd = HEAD_DIM
    n_chunks = seq // c
    group = DN_CHUNKS_PER_ITER
    heads = DN_HEADS_PER_STEP
    lane = lax.broadcasted_iota(jnp.int32, (seq, hd), 1)

    for e in range(heads):
        cols = slice(e * hd, (e + 1) * hd)
        head = pl.program_id(1) * heads + e
        beta = _lane_column(beta_ref[...], lane, head)
        gc = _lane_column(gc_ref[...], lane, head).reshape(n_chunks, c, hd)
        eg = jnp.exp(gc)
        kd = jnp.exp(gc[:, c - 1:c, :] - gc)
        q = q_ref[:, cols]
        k = k_ref[:, cols]
        kb = k * beta
        kdt = (k * kd.reshape(seq, hd)).T
        eg2 = eg.reshape(seq, hd)
        q_s[e] = q.astype(q_s.dtype).reshape(n_chunks, c, hd)
        k_s[e] = k.astype(k_s.dtype).reshape(n_chunks, c, hd)
        kb_s[e] = kb.astype(kb_s.dtype).reshape(n_chunks, c, hd)
        rhs_s[e] = jnp.concatenate([kb * eg2, v_ref[:, cols] * beta], axis=1
                                   ).astype(rhs_s.dtype).reshape(n_chunks, c, 2 * hd)
        qd_s[e] = (q * eg2).reshape(n_chunks, c, hd)
        gc_s[e] = gc
        egl_s[e] = eg[:, c - SUBLANES:, :]
        for n in range(n_chunks):
            kdt_s[e, n] = kdt[:, n * c:(n + 1) * c].astype(kdt_s.dtype)

    ci = lax.broadcasted_iota(jnp.int32, (c, c), 0)
    cj = lax.broadcasted_iota(jnp.int32, (c, c), 1)

    def prep_stages(it):
        chains = [(e, it * group + g) for g in range(group) for e in range(heads)]
        ks = [k_s[e, n] for e, n in chains]
        kbs = [kb_s[e, n] for e, n in chains]
        qs = [q_s[e, n] for e, n in chains]
        gcs = [gc_s[e, n] for e, n in chains]
        rhss = [rhs_s[e, n] for e, n in chains]
        qds = [qd_s[e, n] for e, n in chains]
        kdts = [kdt_s[e, n] for e, n in chains]
        yield
        kks = [_dot_nt(kb, k) for kb, k in zip(kbs, ks)]
        qks = [_dot_nt(q, k) for q, k in zip(qs, ks)]
        yield
        decays = [jnp.exp(jnp.where(ci >= cj, gc - gc.T, -jnp.inf)) for gc in gcs]
        nmats = [jnp.where(ci > cj, kk * decay, 0.0) for kk, decay in zip(kks, decays)]
        tinvs = yield from _unit_lower_inverses(nmats, ci, cj)
        sols = [_dot(tinv, rhs) for tinv, rhs in zip(tinvs, rhss)]
        yield
        lhs = [jnp.concatenate([kdt, (qk * decay).astype(MXU_DTYPE)], axis=0)
               for kdt, qk, decay in zip(kdts, qks, decays)]
        xys = [_dot(l, sol) for l, sol in zip(lhs, sols)]
        yield
        return [(e, n, jnp.concatenate([-xy[:c, :hd], qd - xy[c:, :hd]], axis=0),
                 xy[:c, hd:], xy[c:, hd:]) for (e, n), xy, qd in zip(chains, xys, qds)]

    def store_prepped(prepped):
        for e, n, mc, bmat, dmat in prepped:
            mc_s[e, n] = mc.astype(mc_s.dtype)
            b_s[e, n] = bmat
            d_s[e, n] = dmat

    def scan_stages(it, states):
        r = range(heads)
        loaded = []
        for g in range(group):
            n = it * group + g
            loaded.append(([mc_s[e, n] for e in r], [b_s[e, n] for e in r], [d_s[e, n] for e in r],
                           [egl_s[e, n][SUBLANES - 1:SUBLANES, :] for e in r]))
        yield
        outs = []
        for mcs, bs, ds, egls in loaded:
            prods = [_dot(mcs[e], states[e]) for e in r]
            yield
            outs.append([prods[e][c:, :] + ds[e] for e in r])
            states = tuple(states[e] * egls[e] + prods[e][:c, :] + bs[e] for e in r)
        return states, outs

    def store_outs(it, outs):
        for g, chunk_outs in enumerate(outs):
            for e in range(heads):
                o_s[e, it * group + g] = chunk_outs[e]

    def run(gen):
        return _interleave(gen, None)[0]

    n_groups = n_chunks // group
    store_prepped(run(prep_stages(0)))

    def body(it, states):
        prepped, (states, outs) = _interleave(prep_stages(it), scan_stages(it - 1, states))
        store_prepped(prepped)
        store_outs(it - 1, outs)
        return states

    states = lax.fori_loop(1, n_groups, body, tuple(jnp.zeros((hd, hd), jnp.float32) for _ in range(heads)))
    _, outs = run(scan_stages(n_groups - 1, states))
    store_outs(n_groups - 1, outs)

    for e in range(heads):
        cols = slice(e * hd, (e + 1) * hd)
        o = o_s[e].reshape(seq, hd)
        o = o * lax.rsqrt(jnp.mean(o * o, axis=-1, keepdims=True) + EPS) * gain_ref[...]
        o_ref[:, cols] = (o * z_ref[:, cols]).astype(o_ref.dtype)


def gated_deltanet(qkv, zact, gates, gain, batch, seq, n_heads):
    heads = DN_HEADS_PER_STEP
    c = DN_CHUNK
    n_chunks = seq // c
    assert n_heads % heads == 0 and n_chunks % DN_CHUNKS_PER_ITER == 0
    wide = heads * HEAD_DIM
    blk = lambda off: pl.BlockSpec((seq, wide), lambda b, h: (b, off // heads + h))
    gate_spec = pl.BlockSpec((seq, HEAD_DIM), lambda b, h: (b, 0))
    per_chunk = lambda width, dtype: pltpu.VMEM((heads, n_chunks, c, width), dtype)
    return pl.pallas_call(
        functools.partial(_deltanet_kernel, seq=seq),
        grid=(batch, n_heads // heads),
        in_specs=[blk(0), blk(n_heads), blk(2 * n_heads), blk(0),
                  gate_spec, gate_spec,
                  pl.BlockSpec((1, HEAD_DIM), lambda b, h: (0, 0))],
        out_specs=pl.BlockSpec((seq, wide), lambda b, h: (b, h)),
        out_shape=jax.ShapeDtypeStruct((batch * seq, n_heads * HEAD_DIM), MXU_DTYPE),
        scratch_shapes=[
            per_chunk(HEAD_DIM, MXU_DTYPE),
            per_chunk(HEAD_DIM, MXU_DTYPE),
            per_chunk(HEAD_DIM, MXU_DTYPE),
            per_chunk(2 * HEAD_DIM, MXU_DTYPE),
            per_chunk(HEAD_DIM, jnp.float32),
            per_chunk(c, MXU_DTYPE),
            per_chunk(HEAD_DIM, jnp.float32),
            pltpu.VMEM((heads, n_chunks, SUBLANES, HEAD_DIM), jnp.float32),
            pltpu.VMEM((heads, n_chunks, 2 * c, HEAD_DIM), MXU_DTYPE),
            per_chunk(HEAD_DIM, jnp.float32),
            per_chunk(HEAD_DIM, jnp.float32),
            per_chunk(HEAD_DIM, jnp.float32),
        ],
        compiler_params=_cparams("parallel", "parallel"),
        name="gated_deltanet",
    )(qkv, qkv, qkv, zact, *gates, gain)


def kernel(x, ffn1_norm, ffn1_w_gate, ffn1_w_up, ffn1_w_down, mix_norm, w_in, conv_w, a_log, dt_bias,
           dn_norm, w_out, ffn2_norm, ffn2_w_gate, ffn2_w_up, ffn2_w_down, final_norm):
    batch, seq, d_model = x.shape
    n_heads = a_log.shape[0]
    d_head_group = n_heads * HEAD_DIM
    h = x.reshape(batch * seq, d_model)

    h, xn = ffn(h, ffn1_norm, ffn1_w_gate, ffn1_w_up, ffn1_w_down, mix_norm, True, MXU_DTYPE)

    wt = w_in.T
    qkv_attn = matmul_nt(xn, wt, 0, 3 * d_head_group, MXU_DTYPE, lead_cols=d_head_group, lead_scale=ATTN_Q_SCALE)
    qkv_dn = proj_conv_qkv(xn, wt, conv_w, 3 * d_head_group, seq)
    zact = matmul_nt(xn, wt, 6 * d_head_group, d_head_group, jnp.float32, act=_silu)
    n_main = 7 * d_head_group
    rows_pad = lambda t: jnp.pad(t.astype(jnp.float32), ((0, HEAD_DIM - n_heads), (0, 0)))
    lane_pad = lambda t: jnp.pad(t.astype(jnp.float32).reshape(1, n_heads), ((0, 0), (0, HEAD_DIM - n_heads)))
    gates = dn_gates(xn, rows_pad(wt[n_main:n_main + n_heads]), rows_pad(wt[n_main + n_heads:]),
                     lane_pad(a_log), lane_pad(dt_bias))
    attn = dilated_attention(qkv_attn, batch, seq, n_heads)
    dn = gated_deltanet(qkv_dn, zact, gates, dn_norm.astype(jnp.float32).reshape(1, HEAD_DIM),
                        batch, seq, n_heads)
    h = matmul2_residual(attn, dn, w_out, h)

    out = ffn(h, ffn2_norm, ffn2_w_gate, ffn2_w_up, ffn2_w_down, final_norm, False, jnp.float32)
    return out.reshape(batch, seq, d_model)
```

```python
import functools
import math

import jax
import jax.numpy as jnp
from jax import lax
from jax.experimental import pallas as pl
from jax.experimental.pallas import tpu as pltpu

EPS = 1e-6
HEAD_DIM = 128
DILATED_CONFIGS = ((128, 1), (512, 4), (2048, 16))
CONV_WIDTH = 4
DN_CHUNK = 128
DN_HEADS_PER_STEP = 2
DN_CHUNKS_PER_ITER = 8
MASKED_BIAS = -1e30
ATTN_Q_SCALE = HEAD_DIM ** -0.5 * math.log2(math.e)
SUBLANES = 8

VMEM_LIMIT_BYTES = 56 * 1024 * 1024
FFN_VMEM_LIMIT_BYTES = 60 * 1024 * 1024
MXU_DTYPE = jnp.bfloat16


def _cparams(*sem):
    return pltpu.CompilerParams(dimension_semantics=sem, vmem_limit_bytes=VMEM_LIMIT_BYTES)


def _dot(a, b):
    return jnp.dot(a.astype(MXU_DTYPE), b.astype(MXU_DTYPE), preferred_element_type=jnp.float32)


def _dot_nt(a, b):
    return lax.dot_general(a.astype(MXU_DTYPE), b.astype(MXU_DTYPE), (((1,), (1,)), ((), ())),
                           preferred_element_type=jnp.float32)


def _silu(x):
    hx = 0.5 * x
    return hx + hx * jnp.tanh(hx)


PROJ_CONV_ROW_CHUNK = 512
FFN_ROW_CHUNK = 32
FFN_NORM_PIECES = 4
FFN_IN_CHUNKS = 8
FFN_K_CHUNK = 1024
FFN_N_CHUNK = 512


def _rmsnorm_rows(src_ref, w_ref, dst_ref, row0, n_rows):
    w = w_ref[...]
    step_rows = FFN_ROW_CHUNK * FFN_NORM_PIECES

    def body(r, carry):
        pieces = [pl.ds(pl.multiple_of(row0 + r * step_rows + p * FFN_ROW_CHUNK, FFN_ROW_CHUNK), FFN_ROW_CHUNK)
                  for p in range(FFN_NORM_PIECES)]
        sq = [src_ref[rows, :] for rows in pieces]
        inv = [lax.rsqrt(jnp.mean(x * x, axis=-1, keepdims=True) + EPS) for x in sq]
        for rows, scale in zip(pieces, inv):
            dst_ref[rows, :] = ((src_ref[rows, :] * scale) * w).astype(dst_ref.dtype)
        return carry
    lax.fori_loop(0, n_rows // step_rows, body, 0)


def _ffn_kernel(h_hbm, nw_ref, wg_ref, wu_ref, wd_ref, onw_ref, *rest, emit_res):
    if emit_res:
        res_hbm, norm_hbm, acc, xn_s, sem_in, sem_out = rest
    else:
        norm_hbm, acc, xn_s, sem_in, sem_out = rest
    i, j = pl.program_id(0), pl.program_id(1)
    n_i, n_j = pl.num_programs(0), pl.num_programs(1)
    bm, d = acc.shape
    rows_in = bm // FFN_IN_CHUNKS
    tile = lambda ref, t: ref.at[pl.ds(pl.multiple_of(t * bm, bm), bm), :]

    def in_copy(t, c):
        part = pl.ds(pl.multiple_of(c * rows_in, rows_in), rows_in)
        return pltpu.make_async_copy(tile(h_hbm, t).at[part, :], acc.at[part, :], sem_in.at[c])

    res_copy = lambda t: pltpu.make_async_copy(acc, tile(res_hbm, t), sem_out.at[0])
    norm_copy = lambda t: pltpu.make_async_copy(xn_s if emit_res else acc, tile(norm_hbm, t), sem_out.at[1])

    @pl.when(j == 0)
    def _():
        @pl.when(i > 0)
        def _():
            (res_copy if emit_res else norm_copy)(i - 1).wait()

        for c in range(FFN_IN_CHUNKS):
            in_copy(i, c).start()
        if emit_res:
            @pl.when(i > 0)
            def _():
                norm_copy(i - 1).wait()

        def land(c, carry):
            in_copy(i, c).wait()
            _rmsnorm_rows(acc, nw_ref, xn_s, c * rows_in, rows_in)
            return carry
        lax.fori_loop(0, FFN_IN_CHUNKS, land, 0)

    g = u = None
    for kc in range(d // FFN_K_CHUNK):
        ks = slice(kc * FFN_K_CHUNK, (kc + 1) * FFN_K_CHUNK)
        xk = xn_s[:, ks]
        gk, uk = _dot(xk, wg_ref[ks, :]), _dot(xk, wu_ref[ks, :])
        g, u = (gk, uk) if g is None else (g + gk, u + uk)
    act = ((0.5 * _silu(g)) * u).astype(MXU_DTYPE)
    for nc in range(d // FFN_N_CHUNK):
        ns = slice(nc * FFN_N_CHUNK, (nc + 1) * FFN_N_CHUNK)
        acc[:, ns] += _dot(act, wd_ref[:, ns])

    @pl.when(j == n_j - 1)
    def _():
        if emit_res:
            res_copy(i).start()
            _rmsnorm_rows(acc, onw_ref, xn_s, 0, bm)
        else:
            _rmsnorm_rows(acc, onw_ref, acc, 0, bm)
        norm_copy(i).start()

        @pl.when(i == n_i - 1)
        def _():
            if emit_res:
                res_copy(i).wait()
            norm_copy(i).wait()


def ffn(h, norm_w, w_gate, w_up, w_down, out_norm_w, emit_res, norm_dtype, bm=1024, bf=256):
    t, d = h.shape
    f = w_gate.shape[1]
    assert t % bm == 0 and f % bf == 0 and d % FFN_K_CHUNK == 0 and d % FFN_N_CHUNK == 0
    assert bm % (FFN_IN_CHUNKS * FFN_ROW_CHUNK * FFN_NORM_PIECES) == 0
    assert emit_res or norm_dtype == jnp.float32
    any_spec = pl.BlockSpec(memory_space=pl.ANY)
    row_spec = pl.BlockSpec((1, d), lambda i, j: (0, 0))
    norm_shape = jax.ShapeDtypeStruct((t, d), norm_dtype)
    return pl.pallas_call(
        functools.partial(_ffn_kernel, emit_res=emit_res),
        grid=(t // bm, f // bf),
        in_specs=[any_spec, row_spec,
                  pl.BlockSpec((d, bf), lambda i, j: (0, j)),
                  pl.BlockSpec((d, bf), lambda i, j: (0, j)),
                  pl.BlockSpec((bf, d), lambda i, j: (j, 0)),
                  row_spec],
        out_specs=[any_spec, any_spec] if emit_res else any_spec,
        out_shape=[jax.ShapeDtypeStruct((t, d), jnp.float32), norm_shape] if emit_res else norm_shape,
        scratch_shapes=[pltpu.VMEM((bm, d), jnp.float32), pltpu.VMEM((bm, d), MXU_DTYPE),
                        pltpu.SemaphoreType.DMA((FFN_IN_CHUNKS,)), pltpu.SemaphoreType.DMA((2,))],
        compiler_params=pltpu.CompilerParams(dimension_semantics=("arbitrary", "arbitrary"),
                                             vmem_limit_bytes=FFN_VMEM_LIMIT_BYTES),
        name="ffn",
    )(h, norm_w.reshape(1, d), w_gate, w_up, w_down, out_norm_w.reshape(1, d))


def _mm_kernel(a_ref, w_ref, o_ref, *, act, lead_blocks, lead_scale):
    acc = _dot_nt(a_ref[...], w_ref[...])
    if lead_blocks:
        acc = acc * jnp.where(pl.program_id(1) < lead_blocks, lead_scale, 1.0)
    o_ref[...] = (act(acc) if act else acc).astype(o_ref.dtype)


def matmul_nt(a, wt, row0, n, out_dtype, act=None, lead_cols=0, lead_scale=1.0, bm=2048, bn=512):
    t, k = a.shape
    assert row0 % bn == 0 and n % bn == 0 and lead_cols % bn == 0
    return pl.pallas_call(
        functools.partial(_mm_kernel, act=act, lead_blocks=lead_cols // bn, lead_scale=lead_scale),
        grid=(t // bm, n // bn),
        in_specs=[pl.BlockSpec((bm, k), lambda i, j: (i, 0), pipeline_mode=pl.Buffered(1)),
                  pl.BlockSpec((bn, k), lambda i, j: (row0 // bn + j, 0))],
        out_specs=pl.BlockSpec((bm, bn), lambda i, j: (i, j)),
        out_shape=jax.ShapeDtypeStruct((t, n), out_dtype),
        compiler_params=_cparams("parallel", "parallel"),
        name="matmul_nt",
    )(a, wt)


def _proj_conv_kernel(a_ref, w_ref, cw_ref, o_ref, acc_s, *, q_blocks, k_blocks):
    seq, bn = o_ref.shape
    chunk = min(PROJ_CONV_ROW_CHUNK, seq)
    w = w_ref[...].astype(MXU_DTYPE)
    cw = cw_ref[...]
    j = pl.program_id(1)
    is_v = j >= q_blocks + k_blocks
    qk_scale = jnp.where(j < q_blocks, HEAD_DIM ** -0.5, 1.0)
    row = lax.broadcasted_iota(jnp.int32, (SUBLANES, bn), 0)
    acc_s[:SUBLANES, :] = jnp.zeros((SUBLANES, bn), jnp.float32)
    for r in range(seq // chunk):
        acc_s[SUBLANES + r * chunk:SUBLANES + (r + 1) * chunk, :] = _dot_nt(a_ref[r * chunk:(r + 1) * chunk, :], w)
    for r in range(seq // chunk):
        y = None
        for i in range(CONV_WIDTH):
            start = SUBLANES + r * chunk - (CONV_WIDTH - 1 - i)
            term = cw[i:i + 1, :] * acc_s[start:start + chunk, :]
            y = term if y is None else y + term
        y = _silu(y)
        heads = []
        for g in range(bn // HEAD_DIM):
            yg = y[:, g * HEAD_DIM:(g + 1) * HEAD_DIM]
            inv = lax.rsqrt(jnp.sum(yg * yg, axis=-1, keepdims=True) + EPS) * qk_scale
            heads.append(yg * jnp.where(is_v, 1.0, inv))
        o_ref[r * chunk:(r + 1) * chunk, :] = jnp.concatenate(heads, axis=1)


def proj_conv_qkv(a, wt, conv_w, row0, seq, bn=256):
    t, k = a.shape
    n = conv_w.shape[1]
    width = n // 3
    assert row0 % bn == 0 and width % bn == 0 and bn % HEAD_DIM == 0
    return pl.pallas_call(
        functools.partial(_proj_conv_kernel, q_blocks=width // bn, k_blocks=width // bn),
        grid=(t // seq, n // bn),
        in_specs=[pl.BlockSpec((seq, k), lambda i, j: (i, 0)),
                  pl.BlockSpec((bn, k), lambda i, j: (row0 // bn + j, 0)),
                  pl.BlockSpec((CONV_WIDTH, bn), lambda i, j: (0, j))],
        out_specs=pl.BlockSpec((seq, bn), lambda i, j: (i, j)),
        out_shape=jax.ShapeDtypeStruct((t, n), jnp.float32),
        scratch_shapes=[pltpu.VMEM((SUBLANES + seq, bn), jnp.float32)],
        compiler_params=_cparams("parallel", "parallel"),
        name="proj_conv_qkv",
    )(a, wt, conv_w)


def _mm2_res_kernel(a1_ref, a2_ref, w1_ref, w2_ref, r_ref, o_ref):
    o_ref[...] = r_ref[...] + (_dot(a1_ref[...], w1_ref[...]) + _dot(a2_ref[...], w2_ref[...]))


def matmul2_residual(a1, a2, w, res, bm=512, bn=1024):
    t, k1 = a1.shape
    assert a2.shape[1] == k1 and w.shape[0] == 2 * k1
    n = w.shape[1]
    return pl.pallas_call(
        _mm2_res_kernel,
        grid=(n // bn, t // bm),
        in_specs=[pl.BlockSpec((bm, k1), lambda j, i: (i, 0)),
                  pl.BlockSpec((bm, k1), lambda j, i: (i, 0)),
                  pl.BlockSpec((k1, bn), lambda j, i: (0, j)),
                  pl.BlockSpec((k1, bn), lambda j, i: (1, j)),
                  pl.BlockSpec((bm, bn), lambda j, i: (i, j))],
        out_specs=pl.BlockSpec((bm, bn), lambda j, i: (i, j)),
        out_shape=jax.ShapeDtypeStruct((t, n), jnp.float32),
        compiler_params=_cparams("parallel", "parallel"),
        name="matmul2_residual",
    )(a1, a2, w, w, res)


def _attn_kernel(q_ref, k_ref, v_ref, o_ref, bias_ref, *, seq, tq):
    @pl.when((pl.program_id(0) == 0) & (pl.program_id(1) == 0))
    def _():
        i = lax.broadcasted_iota(jnp.int32, (tq, seq), 0)
        jj = lax.broadcasted_iota(jnp.int32, (tq, seq), 1)
        dist = i - (jj - (seq - tq))
        count = jnp.zeros((tq, seq), jnp.int32)
        for window, d in DILATED_CONFIGS:
            hit = (dist >= 0) & (dist <= (window // d) * d) & ((dist & (d - 1)) == 0)
            count = count + hit.astype(jnp.int32)
        bias_ref[...] = jnp.where(
            count == 3, math.log2(3.0),
            jnp.where(count == 2, 1.0, jnp.where(count == 1, 0.0, MASKED_BIAS)))

    k = k_ref[...].astype(MXU_DTYPE)
    v = v_ref[...].astype(MXU_DTYPE)
    n_tiles = seq // tq

    def scores(qi):
        kend = (qi + 1) * tq
        return _dot_nt(q_ref[qi * tq:kend, :], k[:kend]) + bias_ref[:, seq - kend:]

    s = scores(0)
    for qi in range(n_tiles):
        kend = (qi + 1) * tq
        s_next = scores(qi + 1) if qi + 1 < n_tiles else None
        m = jnp.max(s, axis=-1, keepdims=True)
        p = jnp.exp2(s - m)
        den = jnp.sum(p, axis=-1, keepdims=True)
        o = _dot(p, v[:kend])
        o_ref[qi * tq:kend, :] = (o / den).astype(o_ref.dtype)
        s = s_next


def dilated_attention(proj, batch, seq, n_heads, tq=256):
    tq = min(tq, seq)
    for window, d in DILATED_CONFIGS:
        assert d & (d - 1) == 0
    blk = lambda off: pl.BlockSpec((seq, HEAD_DIM), lambda b, h: (b, off + h))
    return pl.pallas_call(
        functools.partial(_attn_kernel, seq=seq, tq=tq),
        grid=(batch, n_heads),
        in_specs=[blk(0), blk(n_heads), blk(2 * n_heads)],
        out_specs=pl.BlockSpec((seq, HEAD_DIM), lambda b, h: (b, h)),
        out_shape=jax.ShapeDtypeStruct((batch * seq, n_heads * HEAD_DIM), MXU_DTYPE),
        scratch_shapes=[pltpu.VMEM((tq, seq), jnp.float32)],
        compiler_params=_cparams("arbitrary", "arbitrary"),
        name="dilated_attention",
    )(proj, proj, proj)


def _dn_gates_kernel(x_ref, wb_ref, wa_ref, alog_ref, dtb_ref, beta_ref, gc_ref):
    c = DN_CHUNK
    x = x_ref[...]
    beta_ref[...] = jax.nn.sigmoid(_dot_nt(x, wb_ref[...]))
    z = _dot_nt(x, wa_ref[...]) + dtb_ref[...]
    softplus = jnp.maximum(z, 0.0) + jnp.log1p(jnp.exp(-jnp.abs(z)))
    gc = -jnp.exp(alog_ref[...]) * softplus
    row = lax.broadcasted_iota(jnp.int32, gc.shape, 0)
    s = 1
    while s < c:
        gc = gc + jnp.where((row & (c - 1)) >= s, pltpu.roll(gc, s, axis=0), 0.0)
        s *= 2
    gc_ref[...] = gc


def dn_gates(xn, wt_beta, wt_decay, a_row, dt_row, bm=1024):
    t, d = xn.shape
    out = jax.ShapeDtypeStruct((t, HEAD_DIM), jnp.float32)
    tok = pl.BlockSpec((bm, HEAD_DIM), lambda i: (i, 0))
    wspec = pl.BlockSpec((HEAD_DIM, d), lambda i: (0, 0))
    row_spec = pl.BlockSpec((1, HEAD_DIM), lambda i: (0, 0))
    return pl.pallas_call(
        _dn_gates_kernel,
        grid=(t // bm,),
        in_specs=[pl.BlockSpec((bm, d), lambda i: (i, 0)), wspec, wspec, row_spec, row_spec],
        out_specs=[tok, tok],
        out_shape=[out, out],
        compiler_params=_cparams("parallel"),
        name="dn_gates",
    )(xn, wt_beta, wt_decay, a_row, dt_row)


def _lane_column(x, lane, col):
    picked = jnp.sum(jnp.where(lane == col, x, 0.0), axis=-1, keepdims=True)
    return jnp.broadcast_to(picked, x.shape)


def _unit_lower_inverses(nmats, ci, cj):
    c = nmats[0].shape[0]
    eye = (ci == cj).astype(jnp.float32)
    same8 = (ci >> 3) == (cj >> 3)
    ms = [-jnp.where(same8, nmat, 0.0) for nmat in nmats]
    tinvs = [eye + m for m in ms]
    ms = [_dot(m, m) for m in ms]
    yield
    xs = [_dot(m, jnp.concatenate([m, tinv], axis=1)) for m, tinv in zip(ms, tinvs)]
    yield
    tinvs = [tinv + x[:, c:] for tinv, x in zip(tinvs, xs)]
    ys = [_dot(x[:, :c], tinv) for x, tinv in zip(xs, tinvs)]
    yield
    tinvs = [tinv + y for tinv, y in zip(tinvs, ys)]
    shift = 3
    while (1 << shift) < c:
        pick = ((ci >> (shift + 1)) == (cj >> (shift + 1))) & ((ci >> shift) != (cj >> shift))
        ys = [_dot(jnp.where(pick, nmat, 0.0), tinv) for nmat, tinv in zip(nmats, tinvs)]
        yield
        ys = [_dot(tinv, y) for tinv, y in zip(tinvs, ys)]
        yield
        tinvs = [tinv - y for tinv, y in zip(tinvs, ys)]
        shift += 1
    return tinvs


def _interleave(first, second):
    results = [None, None]
    live = [first, second]
    while any(g is not None for g in live):
        for idx, g in enumerate(live):
            if g is None:
                continue
            try:
                next(g)
            except StopIteration as stop:
                results[idx] = stop.value
                live[idx] = None
    return results


def _deltanet_kernel(q_ref, k_ref, v_ref, z_ref, beta_ref, gc_ref, gain_ref, o_ref,
                     q_s, k_s, kb_s, rhs_s, qd_s, kdt_s, gc_s, egl_s, mc_s, b_s, d_s, o_s, *, seq):
    c = DN_CHUNK
    hd = HEAD_DIM
    n_chunks = seq // c
    group = DN_CHUNKS_PER_ITER
    heads = DN_HEADS_PER_STEP
    lane = lax.broadcasted_iota(jnp.int32, (seq, hd), 1)

    for e in range(heads):
        cols = slice(e * hd, (e + 1) * hd)
        head = pl.program_id(1) * heads + e
        beta = _lane_column(beta_ref[...], lane, head)
        gc = _lane_column(gc_ref[...], lane, head).reshape(n_chunks, c, hd)
        eg = jnp.exp(gc)
        kd = jnp.exp(gc[:, c - 1:c, :] - gc)
        q = q_ref[:, cols]
        k = k_ref[:, cols]
        kb = k * beta
        kdt = (k * kd.reshape(seq, hd)).T
        eg2 = eg.reshape(seq, hd)
        q_s[e] = q.astype(q_s.dtype).reshape(n_chunks, c, hd)
        k_s[e] = k.astype(k_s.dtype).reshape(n_chunks, c, hd)
        kb_s[e] = kb.astype(kb_s.dtype).reshape(n_chunks, c, hd)
        rhs_s[e] = jnp.concatenate([kb * eg2, v_ref[:, cols] * beta], axis=1
                                   ).astype(rhs_s.dtype).reshape(n_chunks, c, 2 * hd)
        qd_s[e] = (q * eg2).reshape(n_chunks, c, hd)
        gc_s[e] = gc
        egl_s[e] = eg[:, c - SUBLANES:, :]
        for n in range(n_chunks):
            kdt_s[e, n] = kdt[:, n * c:(n + 1) * c].astype(kdt_s.dtype)

    ci = lax.broadcasted_iota(jnp.int32, (c, c), 0)
    cj = lax.broadcasted_iota(jnp.int32, (c, c), 1)

    def prep_stages(it):
        chains = [(e, it * group + g) for g in range(group) for e in range(heads)]
        ks = [k_s[e, n] for e, n in chains]
        kbs = [kb_s[e, n] for e, n in chains]
        qs = [q_s[e, n] for e, n in chains]
        gcs = [gc_s[e, n] for e, n in chains]
        rhss = [rhs_s[e, n] for e, n in chains]
        qds = [qd_s[e, n] for e, n in chains]
        kdts = [kdt_s[e, n] for e, n in chains]
        yield
        kks = [_dot_nt(kb, k) for kb, k in zip(kbs, ks)]
        qks = [_dot_nt(q, k) for q, k in zip(qs, ks)]
        yield
        decays = [jnp.exp(jnp.where(ci >= cj, gc - gc.T, -jnp.inf)) for gc in gcs]
        nmats = [jnp.where(ci > cj, kk * decay, 0.0) for kk, decay in zip(kks, decays)]
        tinvs = yield from _unit_lower_inverses(nmats, ci, cj)
        sols = [_dot(tinv, rhs) for tinv, rhs in zip(tinvs, rhss)]
        yield
        lhs = [jnp.concatenate([kdt, (qk * decay).astype(MXU_DTYPE)], axis=0)
               for kdt, qk, decay in zip(kdts, qks, decays)]
        xys = [_dot(l, sol) for l, sol in zip(lhs, sols)]
        yield
        return [(e, n, jnp.concatenate([-xy[:c, :hd], qd - xy[c:, :hd]], axis=0),
                 xy[:c, hd:], xy[c:, hd:]) for (e, n), xy, qd in zip(chains, xys, qds)]

    def store_prepped(prepped):
        for e, n, mc, bmat, dmat in prepped:
            mc_s[e, n] = mc.astype(mc_s.dtype)
            b_s[e, n] = bmat
            d_s[e, n] = dmat

    def scan_stages(it, states):
        r = range(heads)
        loaded = []
        for g in range(group):
            n = it * group + g
            loaded.append(([mc_s[e, n] for e in r], [b_s[e, n] for e in r], [d_s[e, n] for e in r],
                           [egl_s[e, n][SUBLANES - 1:SUBLANES, :] for e in r]))
        yield
        outs = []
        for mcs, bs, ds, egls in loaded:
            prods = [_dot(mcs[e], states[e]) for e in r]
            yield
            outs.append([prods[e][c:, :] + ds[e] for e in r])
            states = tuple(states[e] * egls[e] + prods[e][:c, :] + bs[e] for e in r)
        return states, outs

    def store_outs(it, outs):
        for g, chunk_outs in enumerate(outs):
            for e in range(heads):
                o_s[e, it * group + g] = chunk_outs[e]

    def run(gen):
        return _interleave(gen, None)[0]

    n_groups = n_chunks // group
    store_prepped(run(prep_stages(0)))

    def body(it, states):
        prepped, (states, outs) = _interleave(prep_stages(it), scan_stages(it - 1, states))
        store_prepped(prepped)
        store_outs(it - 1, outs)
        return states

    states = lax.fori_loop(1, n_groups, body, tuple(jnp.zeros((hd, hd), jnp.float32) for _ in range(heads)))
    _, outs = run(scan_stages(n_groups - 1, states))
    store_outs(n_groups - 1, outs)

    for e in range(heads):
        cols = slice(e * hd, (e + 1) * hd)
        o = o_s[e].reshape(seq, hd)
        o = o * lax.rsqrt(jnp.mean(o * o, axis=-1, keepdims=True) + EPS) * gain_ref[...]
        o_ref[:, cols] = (o * z_ref[:, cols]).astype(o_ref.dtype)


def gated_deltanet(qkv, zact, gates, gain, batch, seq, n_heads):
    heads = DN_HEADS_PER_STEP
    c = DN_CHUNK
    n_chunks = seq // c
    assert n_heads % heads == 0 and n_chunks % DN_CHUNKS_PER_ITER == 0
    wide = heads * HEAD_DIM
    blk = lambda off: pl.BlockSpec((seq, wide), lambda b, h: (b, off // heads + h))
    gate_spec = pl.BlockSpec((seq, HEAD_DIM), lambda b, h: (b, 0))
    per_chunk = lambda width, dtype: pltpu.VMEM((heads, n_chunks, c, width), dtype)
    return pl.pallas_call(
        functools.partial(_deltanet_kernel, seq=seq),
        grid=(batch, n_heads // heads),
        in_specs=[blk(0), blk(n_heads), blk(2 * n_heads), blk(0),
                  gate_spec, gate_spec,
                  pl.BlockSpec((1, HEAD_DIM), lambda b, h: (0, 0))],
        out_specs=pl.BlockSpec((seq, wide), lambda b, h: (b, h)),
        out_shape=jax.ShapeDtypeStruct((batch * seq, n_heads * HEAD_DIM), MXU_DTYPE),
        scratch_shapes=[
            per_chunk(HEAD_DIM, MXU_DTYPE),
            per_chunk(HEAD_DIM, MXU_DTYPE),
            per_chunk(HEAD_DIM, MXU_DTYPE),
            per_chunk(2 * HEAD_DIM, MXU_DTYPE),
            per_chunk(HEAD_DIM, jnp.float32),
            per_chunk(c, MXU_DTYPE),
            per_chunk(HEAD_DIM, jnp.float32),
            pltpu.VMEM((heads, n_chunks, SUBLANES, HEAD_DIM), jnp.float32),
            pltpu.VMEM((heads, n_chunks, 2 * c, HEAD_DIM), MXU_DTYPE),
            per_chunk(HEAD_DIM, jnp.float32),
            per_chunk(HEAD_DIM, jnp.float32),
            per_chunk(HEAD_DIM, jnp.float32),
        ],
        compiler_params=_cparams("parallel", "parallel"),
        name="gated_deltanet",
    )(qkv, qkv, qkv, zact, *gates, gain)


def kernel(x, ffn1_norm, ffn1_w_gate, ffn1_w_up, ffn1_w_down, mix_norm, w_in, conv_w, a_log, dt_bias,
           dn_norm, w_out, ffn2_norm, ffn2_w_gate, ffn2_w_up, ffn2_w_down, final_norm):
    batch, seq, d_model = x.shape
    n_heads = a_log.shape[0]
    d_head_group = n_heads * HEAD_DIM
    h = x.reshape(batch * seq, d_model)

    h, xn = ffn(h, ffn1_norm, ffn1_w_gate, ffn1_w_up, ffn1_w_down, mix_norm, True, MXU_DTYPE)

    wt = w_in.T
    qkv_attn = matmul_nt(xn, wt, 0, 3 * d_head_group, MXU_DTYPE, lead_cols=d_head_group, lead_scale=ATTN_Q_SCALE)
    qkv_dn = proj_conv_qkv(xn, wt, conv_w, 3 * d_head_group, seq)
    zact = matmul_nt(xn, wt, 6 * d_head_group, d_head_group, jnp.float32, act=_silu)
    n_main = 7 * d_head_group
    rows_pad = lambda t: jnp.pad(t.astype(jnp.float32), ((0, HEAD_DIM - n_heads), (0, 0)))
    lane_pad = lambda t: jnp.pad(t.astype(jnp.float32).reshape(1, n_heads), ((0, 0), (0, HEAD_DIM - n_heads)))
    gates = dn_gates(xn, rows_pad(wt[n_main:n_main + n_heads]), rows_pad(wt[n_main + n_heads:]),
                     lane_pad(a_log), lane_pad(dt_bias))
    attn = dilated_attention(qkv_attn, batch, seq, n_heads)
    dn = gated_deltanet(qkv_dn, zact, gates, dn_norm.astype(jnp.float32).reshape(1, HEAD_DIM),
                        batch, seq, n_heads)
    h = matmul2_residual(attn, dn, w_out, h)

    out = ffn(h, ffn2_norm, ffn2_w_gate, ffn2_w_up, ffn2_w_down, final_norm, False, jnp.float32)
    return out.reshape(batch, seq, d_model)
```

```python
import functools
import math

import jax
import jax.numpy as jnp
from jax import lax
from jax.experimental import pallas as pl
from jax.experimental.pallas import tpu as pltpu

EPS = 1e-6
HEAD_DIM = 128
DILATED_CONFIGS = ((128, 1), (512, 4), (2048, 16))
CONV_WIDTH = 4
DN_CHUNK = 128
DN_HEADS_PER_STEP = 2
DN_CHUNKS_PER_ITER = 8
MASKED_BIAS = -1e30
ATTN_Q_SCALE = HEAD_DIM ** -0.5 * math.log2(math.e)
SUBLANES = 8

VMEM_LIMIT_BYTES = 56 * 1024 * 1024
FFN_VMEM_LIMIT_BYTES = 60 * 1024 * 1024
MXU_DTYPE = jnp.bfloat16


def _cparams(*sem):
    return pltpu.CompilerParams(dimension_semantics=sem, vmem_limit_bytes=VMEM_LIMIT_BYTES)


def _dot(a, b):
    return jnp.dot(a.astype(MXU_DTYPE), b.astype(MXU_DTYPE), preferred_element_type=jnp.float32)


def _dot_nt(a, b):
    return lax.dot_general(a.astype(MXU_DTYPE), b.astype(MXU_DTYPE), (((1,), (1,)), ((), ())),
                           preferred_element_type=jnp.float32)


def _silu(x):
    hx = 0.5 * x
    return hx + hx * jnp.tanh(hx)


PROJ_CONV_ROW_CHUNK = 512
FFN_ROW_CHUNK = 32
FFN_NORM_PIECES = 4
FFN_IN_CHUNKS = 8
FFN_K_CHUNK = 1024
FFN_N_CHUNK = 512


def _rmsnorm_rows(src_ref, w_ref, dst_ref, row0, n_rows):
    w = w_ref[...]
    step_rows = FFN_ROW_CHUNK * FFN_NORM_PIECES

    def body(r, carry):
        pieces = [pl.ds(pl.multiple_of(row0 + r * step_rows + p * FFN_ROW_CHUNK, FFN_ROW_CHUNK), FFN_ROW_CHUNK)
                  for p in range(FFN_NORM_PIECES)]
        sq = [src_ref[rows, :] for rows in pieces]
        inv = [lax.rsqrt(jnp.mean(x * x, axis=-1, keepdims=True) + EPS) for x in sq]
        for rows, scale in zip(pieces, inv):
            dst_ref[rows, :] = ((src_ref[rows, :] * scale) * w).astype(dst_ref.dtype)
        return carry
    lax.fori_loop(0, n_rows // step_rows, body, 0)


def _ffn_kernel(h_hbm, nw_ref, wg_ref, wu_ref, wd_ref, onw_ref, *rest, emit_res):
    if emit_res:
        res_hbm, norm_hbm, acc, xn_s, sem_in, sem_out = rest
    else:
        norm_hbm, acc, xn_s, sem_in, sem_out = rest
    i, j = pl.program_id(0), pl.program_id(1)
    n_i, n_j = pl.num_programs(0), pl.num_programs(1)
    bm, d = acc.shape
    rows_in = bm // FFN_IN_CHUNKS
    tile = lambda ref, t: ref.at[pl.ds(pl.multiple_of(t * bm, bm), bm), :]

    part = lambda c: pl.ds(pl.multiple_of(c * rows_in, rows_in), rows_in)
    normed = xn_s if emit_res else acc

    def in_copy(t, c):
        return pltpu.make_async_copy(tile(h_hbm, t).at[part(c), :], acc.at[part(c), :], sem_in.at[c])

    def norm_copy(t, c):
        return pltpu.make_async_copy(normed.at[part(c), :], tile(norm_hbm, t).at[part(c), :], sem_out.at[1 + c])

    res_copy = lambda t: pltpu.make_async_copy(acc, tile(res_hbm, t), sem_out.at[0])

    @pl.when(j == 0)
    def _():
        if emit_res:
            @pl.when(i > 0)
            def _():
                res_copy(i - 1).wait()
            for c in range(FFN_IN_CHUNKS):
                in_copy(i, c).start()
        else:
            for c in range(FFN_IN_CHUNKS):
                @pl.when(i > 0)
                def _():
                    norm_copy(i - 1, c).wait()
                in_copy(i, c).start()

        def land(c, carry):
            if emit_res:
                @pl.when(i > 0)
                def _():
                    norm_copy(i - 1, c).wait()
            in_copy(i, c).wait()
            _rmsnorm_rows(acc, nw_ref, xn_s, c * rows_in, rows_in)
            return carry
        lax.fori_loop(0, FFN_IN_CHUNKS, land, 0)

    g = u = None
    for kc in range(d // FFN_K_CHUNK):
        ks = slice(kc * FFN_K_CHUNK, (kc + 1) * FFN_K_CHUNK)
        xk = xn_s[:, ks]
        gk, uk = _dot(xk, wg_ref[ks, :]), _dot(xk, wu_ref[ks, :])
        g, u = (gk, uk) if g is None else (g + gk, u + uk)
    act = ((0.5 * _silu(g)) * u).astype(MXU_DTYPE)
    for nc in range(d // FFN_N_CHUNK):
        ns = slice(nc * FFN_N_CHUNK, (nc + 1) * FFN_N_CHUNK)
        acc[:, ns] += _dot(act, wd_ref[:, ns])

    @pl.when(j == n_j - 1)
    def _():
        if emit_res:
            res_copy(i).start()

        def finish(c, carry):
            _rmsnorm_rows(acc, onw_ref, normed, c * rows_in, rows_in)
            norm_copy(i, c).start()
            return carry
        lax.fori_loop(0, FFN_IN_CHUNKS, finish, 0)

        @pl.when(i == n_i - 1)
        def _():
            if emit_res:
                res_copy(i).wait()
            for c in range(FFN_IN_CHUNKS):
                norm_copy(i, c).wait()


def ffn(h, norm_w, w_gate, w_up, w_down, out_norm_w, emit_res, norm_dtype, bm=1024, bf=256):
    t, d = h.shape
    f = w_gate.shape[1]
    assert t % bm == 0 and f % bf == 0 and d % FFN_K_CHUNK == 0 and d % FFN_N_CHUNK == 0
    assert bm % (FFN_IN_CHUNKS * FFN_ROW_CHUNK * FFN_NORM_PIECES) == 0
    assert emit_res or norm_dtype == jnp.float32
    any_spec = pl.BlockSpec(memory_space=pl.ANY)
    row_spec = pl.BlockSpec((1, d), lambda i, j: (0, 0))
    norm_shape = jax.ShapeDtypeStruct((t, d), norm_dtype)
    return pl.pallas_call(
        functools.partial(_ffn_kernel, emit_res=emit_res),
        grid=(t // bm, f // bf),
        in_specs=[any_spec, row_spec,
                  pl.BlockSpec((d, bf), lambda i, j: (0, j)),
                  pl.BlockSpec((d, bf), lambda i, j: (0, j)),
                  pl.BlockSpec((bf, d), lambda i, j: (j, 0)),
                  row_spec],
        out_specs=[any_spec, any_spec] if emit_res else any_spec,
        out_shape=[jax.ShapeDtypeStruct((t, d), jnp.float32), norm_shape] if emit_res else norm_shape,
        scratch_shapes=[pltpu.VMEM((bm, d), jnp.float32), pltpu.VMEM((bm, d), MXU_DTYPE),
                        pltpu.SemaphoreType.DMA((FFN_IN_CHUNKS,)), pltpu.SemaphoreType.DMA((1 + FFN_IN_CHUNKS,))],
        compiler_params=pltpu.CompilerParams(dimension_semantics=("arbitrary", "arbitrary"),
                                             vmem_limit_bytes=FFN_VMEM_LIMIT_BYTES),
        name="ffn",
    )(h, norm_w.reshape(1, d), w_gate, w_up, w_down, out_norm_w.reshape(1, d))


def _mm_kernel(a_ref, w_ref, o_ref, *, act, lead_blocks, lead_scale):
    acc = _dot_nt(a_ref[...], w_ref[...])
    if lead_blocks:
        acc = acc * jnp.where(pl.program_id(1) < lead_blocks, lead_scale, 1.0)
    o_ref[...] = (act(acc) if act else acc).astype(o_ref.dtype)


def matmul_nt(a, wt, row0, n, out_dtype, act=None, lead_cols=0, lead_scale=1.0, bm=1024, bn=512):
    t, k = a.shape
    assert row0 % bn == 0 and n % bn == 0 and lead_cols % bn == 0
    return pl.pallas_call(
        functools.partial(_mm_kernel, act=act, lead_blocks=lead_cols // bn, lead_scale=lead_scale),
        grid=(t // bm, n // bn),
        in_specs=[pl.BlockSpec((bm, k), lambda i, j: (i, 0)),
                  pl.BlockSpec((bn, k), lambda i, j: (row0 // bn + j, 0))],
        out_specs=pl.BlockSpec((bm, bn), lambda i, j: (i, j)),
        out_shape=jax.ShapeDtypeStruct((t, n), out_dtype),
        compiler_params=_cparams("parallel", "parallel"),
        name="matmul_nt",
    )(a, wt)


def _proj_conv_kernel(a_ref, w_ref, cw_ref, o_ref, acc_s, *, q_blocks, k_blocks):
    seq, bn = o_ref.shape
    chunk = min(PROJ_CONV_ROW_CHUNK, seq)
    w = w_ref[...].astype(MXU_DTYPE)
    cw = cw_ref[...]
    j = pl.program_id(1)
    is_v = j >= q_blocks + k_blocks
    qk_scale = jnp.where(j < q_blocks, HEAD_DIM ** -0.5, 1.0)
    row = lax.broadcasted_iota(jnp.int32, (SUBLANES, bn), 0)
    acc_s[:SUBLANES, :] = jnp.zeros((SUBLANES, bn), jnp.float32)
    for r in range(seq // chunk):
        acc_s[SUBLANES + r * chunk:SUBLANES + (r + 1) * chunk, :] = _dot_nt(a_ref[r * chunk:(r + 1) * chunk, :], w)
    for r in range(seq // chunk):
        y = None
        for i in range(CONV_WIDTH):
            start = SUBLANES + r * chunk - (CONV_WIDTH - 1 - i)
            term = cw[i:i + 1, :] * acc_s[start:start + chunk, :]
            y = term if y is None else y + term
        y = _silu(y)
        heads = []
        for g in range(bn // HEAD_DIM):
            yg = y[:, g * HEAD_DIM:(g + 1) * HEAD_DIM]
            inv = lax.rsqrt(jnp.sum(yg * yg, axis=-1, keepdims=True) + EPS) * qk_scale
            heads.append(yg * jnp.where(is_v, 1.0, inv))
        o_ref[r * chunk:(r + 1) * chunk, :] = jnp.concatenate(heads, axis=1)


def proj_conv_qkv(a, wt, conv_w, row0, seq, bn=256):
    t, k = a.shape
    n = conv_w.shape[1]
    width = n // 3
    assert row0 % bn == 0 and width % bn == 0 and bn % HEAD_DIM == 0
    return pl.pallas_call(
        functools.partial(_proj_conv_kernel, q_blocks=width // bn, k_blocks=width // bn),
        grid=(t // seq, n // bn),
        in_specs=[pl.BlockSpec((seq, k), lambda i, j: (i, 0)),
                  pl.BlockSpec((bn, k), lambda i, j: (row0 // bn + j, 0)),
                  pl.BlockSpec((CONV_WIDTH, bn), lambda i, j: (0, j))],
        out_specs=pl.BlockSpec((seq, bn), lambda i, j: (i, j)),
        out_shape=jax.ShapeDtypeStruct((t, n), jnp.float32),
        scratch_shapes=[pltpu.VMEM((SUBLANES + seq, bn), jnp.float32)],
        compiler_params=_cparams("parallel", "parallel"),
        name="proj_conv_qkv",
    )(a, wt, conv_w)


def _mm2_res_kernel(a1_ref, a2_ref, w1_ref, w2_ref, r_ref, o_ref):
    o_ref[...] = r_ref[...] + (_dot(a1_ref[...], w1_ref[...]) + _dot(a2_ref[...], w2_ref[...]))


def matmul2_residual(a1, a2, w, res, bm=512, bn=1024):
    t, k1 = a1.shape
    assert a2.shape[1] == k1 and w.shape[0] == 2 * k1
    n = w.shape[1]
    return pl.pallas_call(
        _mm2_res_kernel,
        grid=(n // bn, t // bm),
        in_specs=[pl.BlockSpec((bm, k1), lambda j, i: (i, 0)),
                  pl.BlockSpec((bm, k1), lambda j, i: (i, 0)),
                  pl.BlockSpec((k1, bn), lambda j, i: (0, j)),
                  pl.BlockSpec((k1, bn), lambda j, i: (1, j)),
                  pl.BlockSpec((bm, bn), lambda j, i: (i, j))],
        out_specs=pl.BlockSpec((bm, bn), lambda j, i: (i, j)),
        out_shape=jax.ShapeDtypeStruct((t, n), jnp.float32),
        compiler_params=_cparams("parallel", "parallel"),
        name="matmul2_residual",
    )(a1, a2, w, w, res)


def _attn_kernel(q_ref, k_ref, v_ref, o_ref, bias_ref, *, seq, tq):
    @pl.when((pl.program_id(0) == 0) & (pl.program_id(1) == 0))
    def _():
        i = lax.broadcasted_iota(jnp.int32, (tq, seq), 0)
        jj = lax.broadcasted_iota(jnp.int32, (tq, seq), 1)
        dist = i - (jj - (seq - tq))
        count = jnp.zeros((tq, seq), jnp.int32)
        for window, d in DILATED_CONFIGS:
            hit = (dist >= 0) & (dist <= (window // d) * d) & ((dist & (d - 1)) == 0)
            count = count + hit.astype(jnp.int32)
        bias_ref[...] = jnp.where(
            count == 3, math.log2(3.0),
            jnp.where(count == 2, 1.0, jnp.where(count == 1, 0.0, MASKED_BIAS)))

    k = k_ref[...].astype(MXU_DTYPE)
    v = v_ref[...].astype(MXU_DTYPE)
    n_tiles = seq // tq

    def scores(qi):
        kend = (qi + 1) * tq
        return _dot_nt(q_ref[qi * tq:kend, :], k[:kend]) + bias_ref[:, seq - kend:]

    s = scores(0)
    for qi in range(n_tiles):
        kend = (qi + 1) * tq
        s_next = scores(qi + 1) if qi + 1 < n_tiles else None
        m = jnp.max(s, axis=-1, keepdims=True)
        p = jnp.exp2(s - m)
        den = jnp.sum(p, axis=-1, keepdims=True)
        o = _dot(p, v[:kend])
        o_ref[qi * tq:kend, :] = (o / den).astype(o_ref.dtype)
        s = s_next


def dilated_attention(proj, batch, seq, n_heads, tq=256):
    tq = min(tq, seq)
    for window, d in DILATED_CONFIGS:
        assert d & (d - 1) == 0
    blk = lambda off: pl.BlockSpec((seq, HEAD_DIM), lambda b, h: (b, off + h))
    return pl.pallas_call(
        functools.partial(_attn_kernel, seq=seq, tq=tq),
        grid=(batch, n_heads),
        in_specs=[blk(0), blk(n_heads), blk(2 * n_heads)],
        out_specs=pl.BlockSpec((seq, HEAD_DIM), lambda b, h: (b, h)),
        out_shape=jax.ShapeDtypeStruct((batch * seq, n_heads * HEAD_DIM), MXU_DTYPE),
        scratch_shapes=[pltpu.VMEM((tq, seq), jnp.float32)],
        compiler_params=_cparams("arbitrary", "arbitrary"),
        name="dilated_attention",
    )(proj, proj, proj)


def _dn_gates_kernel(x_ref, wb_ref, wa_ref, alog_ref, dtb_ref, beta_ref, gc_ref):
    c = DN_CHUNK
    x = x_ref[...]
    beta_ref[...] = jax.nn.sigmoid(_dot_nt(x, wb_ref[...]))
    z = _dot_nt(x, wa_ref[...]) + dtb_ref[...]
    softplus = jnp.maximum(z, 0.0) + jnp.log1p(jnp.exp(-jnp.abs(z)))
    gc = -jnp.exp(alog_ref[...]) * softplus
    row = lax.broadcasted_iota(jnp.int32, gc.shape, 0)
    s = 1
    while s < c:
        gc = gc + jnp.where((row & (c - 1)) >= s, pltpu.roll(gc, s, axis=0), 0.0)
        s *= 2
    gc_ref[...] = gc


def dn_gates(xn, wt_beta, wt_decay, a_row, dt_row, bm=1024):
    t, d = xn.shape
    out = jax.ShapeDtypeStruct((t, HEAD_DIM), jnp.float32)
    tok = pl.BlockSpec((bm, HEAD_DIM), lambda i: (i, 0))
    wspec = pl.BlockSpec((HEAD_DIM, d), lambda i: (0, 0))
    row_spec = pl.BlockSpec((1, HEAD_DIM), lambda i: (0, 0))
    return pl.pallas_call(
        _dn_gates_kernel,
        grid=(t // bm,),
        in_specs=[pl.BlockSpec((bm, d), lambda i: (i, 0)), wspec, wspec, row_spec, row_spec],
        out_specs=[tok, tok],
        out_shape=[out, out],
        compiler_params=_cparams("parallel"),
        name="dn_gates",
    )(xn, wt_beta, wt_decay, a_row, dt_row)


def _lane_column(x, lane, col):
    picked = jnp.sum(jnp.where(lane == col, x, 0.0), axis=-1, keepdims=True)
    return jnp.broadcast_to(picked, x.shape)


def _unit_lower_inverses(nmats, ci, cj):
    c = nmats[0].shape[0]
    eye = (ci == cj).astype(jnp.float32)
    same8 = (ci >> 3) == (cj >> 3)
    ms = [-jnp.where(same8, nmat, 0.0) for nmat in nmats]
    tinvs = [eye + m for m in ms]
    ms = [_dot(m, m) for m in ms]
    yield
    xs = [_dot(m, jnp.concatenate([m, tinv], axis=1)) for m, tinv in zip(ms, tinvs)]
    yield
    tinvs = [tinv + x[:, c:] for tinv, x in zip(tinvs, xs)]
    ys = [_dot(x[:, :c], tinv) for x, tinv in zip(xs, tinvs)]
    yield
    tinvs = [tinv + y for tinv, y in zip(tinvs, ys)]
    shift = 3
    while (1 << shift) < c:
        pick = ((ci >> (shift + 1)) == (cj >> (shift + 1))) & ((ci >> shift) != (cj >> shift))
        ys = [_dot(jnp.where(pick, nmat, 0.0), tinv) for nmat, tinv in zip(nmats, tinvs)]
        yield
        ys = [_dot(tinv, y) for tinv, y in zip(tinvs, ys)]
        yield
        tinvs = [tinv - y for tinv, y in zip(tinvs, ys)]
        shift += 1
    return tinvs


def _interleave(first, second):
    results = [None, None]
    live = [first, second]
    while any(g is not None for g in live):
        for idx, g in enumerate(live):
            if g is None:
                continue
            try:
                next(g)
            except StopIteration as stop:
                results[idx] = stop.value
                live[idx] = None
    return results


def _deltanet_kernel(q_ref, k_ref, v_ref, z_ref, beta_ref, gc_ref, gain_ref, o_ref,
                     q_s, k_s, kb_s, rhs_s, qd_s, kdt_s, gc_s, egl_s, mc_s, b_s, d_s, o_s, *, seq):
    c = DN_CHUNK
    hd = HEAD_DIM
    n_chunks = seq // c
    group = DN_CHUNKS_PER_ITER
    heads = DN_HEADS_PER_STEP
    lane = lax.broadcasted_iota(jnp.int32, (seq, hd), 1)

    for e in range(heads):
        cols = slice(e * hd, (e + 1) * hd)
        head = pl.program_id(1) * heads + e
        beta = _lane_column(beta_ref[...], lane, head)
        gc = _lane_column(gc_ref[...], lane, head).reshape(n_chunks, c, hd)
        eg = jnp.exp(gc)
        kd = jnp.exp(gc[:, c - 1:c, :] - gc)
        q = q_ref[:, cols]
        k = k_ref[:, cols]
        kb = k * beta
        kdt = (k * kd.reshape(seq, hd)).T
        eg2 = eg.reshape(seq, hd)
        q_s[e] = q.astype(q_s.dtype).reshape(n_chunks, c, hd)
        k_s[e] = k.astype(k_s.dtype).reshape(n_chunks, c, hd)
        kb_s[e] = kb.astype(kb_s.dtype).reshape(n_chunks, c, hd)
        rhs_s[e] = jnp.concatenate([kb * eg2, v_ref[:, cols] * beta], axis=1
                                   ).astype(rhs_s.dtype).reshape(n_chunks, c, 2 * hd)
        qd_s[e] = (q * eg2).reshape(n_chunks, c, hd)
        gc_s[e] = gc
        egl_s[e] = eg[:, c - SUBLANES:, :]
        for n in range(n_chunks):
            kdt_s[e, n] = kdt[:, n * c:(n + 1) * c].astype(kdt_s.dtype)

    ci = lax.broadcasted_iota(jnp.int32, (c, c), 0)
    cj = lax.broadcasted_iota(jnp.int32, (c, c), 1)

    def prep_stages(it):
        chains = [(e, it * group + g) for g in range(group) for e in range(heads)]
        ks = [k_s[e, n] for e, n in chains]
        kbs = [kb_s[e, n] for e, n in chains]
        qs = [q_s[e, n] for e, n in chains]
        gcs = [gc_s[e, n] for e, n in chains]
        rhss = [rhs_s[e, n] for e, n in chains]
        qds = [qd_s[e, n] for e, n in chains]
        kdts = [kdt_s[e, n] for e, n in chains]
        yield
        kks = [_dot_nt(kb, k) for kb, k in zip(kbs, ks)]
        qks = [_dot_nt(q, k) for q, k in zip(qs, ks)]
        yield
        decays = [jnp.exp(jnp.where(ci >= cj, gc - gc.T, -jnp.inf)) for gc in gcs]
        nmats = [jnp.where(ci > cj, kk * decay, 0.0) for kk, decay in zip(kks, decays)]
        tinvs = yield from _unit_lower_inverses(nmats, ci, cj)
        sols = [_dot(tinv, rhs) for tinv, rhs in zip(tinvs, rhss)]
        yield
        lhs = [jnp.concatenate([kdt, (qk * decay).astype(MXU_DTYPE)], axis=0)
               for kdt, qk, decay in zip(kdts, qks, decays)]
        xys = [_dot(l, sol) for l, sol in zip(lhs, sols)]
        yield
        return [(e, n, jnp.concatenate([-xy[:c, :hd], qd - xy[c:, :hd]], axis=0),
                 xy[:c, hd:], xy[c:, hd:]) for (e, n), xy, qd in zip(chains, xys, qds)]

    def store_prepped(prepped):
        for e, n, mc, bmat, dmat in prepped:
            mc_s[e, n] = mc.astype(mc_s.dtype)
            b_s[e, n] = bmat
            d_s[e, n] = dmat

    def scan_stages(it, states):
        r = range(heads)
        loaded = []
        for g in range(group):
            n = it * group + g
            loaded.append(([mc_s[e, n] for e in r], [b_s[e, n] for e in r], [d_s[e, n] for e in r],
                           [egl_s[e, n][SUBLANES - 1:SUBLANES, :] for e in r]))
        yield
        outs = []
        for mcs, bs, ds, egls in loaded:
            prods = [_dot(mcs[e], states[e]) for e in r]
            yield
            outs.append([prods[e][c:, :] + ds[e] for e in r])
            states = tuple(states[e] * egls[e] + prods[e][:c, :] + bs[e] for e in r)
        return states, outs

    def store_outs(it, outs):
        for g, chunk_outs in enumerate(outs):
            for e in range(heads):
                o_s[e, it * group + g] = chunk_outs[e]

    def run(gen):
        return _interleave(gen, None)[0]

    n_groups = n_chunks // group
    store_prepped(run(prep_stages(0)))

    def body(it, states):
        prepped, (states, outs) = _interleave(prep_stages(it), scan_stages(it - 1, states))
        store_prepped(prepped)
        store_outs(it - 1, outs)
        return states

    states = lax.fori_loop(1, n_groups, body, tuple(jnp.zeros((hd, hd), jnp.float32) for _ in range(heads)))
    _, outs = run(scan_stages(n_groups - 1, states))
    store_outs(n_groups - 1, outs)

    for e in range(heads):
        cols = slice(e * hd, (e + 1) * hd)
        o = o_s[e].reshape(seq, hd)
        o = o * lax.rsqrt(jnp.mean(o * o, axis=-1, keepdims=True) + EPS) * gain_ref[...]
        o_ref[:, cols] = (o * z_ref[:, cols]).astype(o_ref.dtype)


def gated_deltanet(qkv, zact, gates, gain, batch, seq, n_heads):
    heads = DN_HEADS_PER_STEP
    c = DN_CHUNK
    n_chunks = seq // c
    assert n_heads % heads == 0 and n_chunks % DN_CHUNKS_PER_ITER == 0
    wide = heads * HEAD_DIM
    blk = lambda off: pl.BlockSpec((seq, wide), lambda b, h: (b, off // heads + h))
    gate_spec = pl.BlockSpec((seq, HEAD_DIM), lambda b, h: (b, 0))
    per_chunk = lambda width, dtype: pltpu.VMEM((heads, n_chunks, c, width), dtype)
    return pl.pallas_call(
        functools.partial(_deltanet_kernel, seq=seq),
        grid=(batch, n_heads // heads),
        in_specs=[blk(0), blk(n_heads), blk(2 * n_heads), blk(0),
                  gate_spec, gate_spec,
                  pl.BlockSpec((1, HEAD_DIM), lambda b, h: (0, 0))],
        out_specs=pl.BlockSpec((seq, wide), lambda b, h: (b, h)),
        out_shape=jax.ShapeDtypeStruct((batch * seq, n_heads * HEAD_DIM), MXU_DTYPE),
        scratch_shapes=[
            per_chunk(HEAD_DIM, MXU_DTYPE),
            per_chunk(HEAD_DIM, MXU_DTYPE),
            per_chunk(HEAD_DIM, MXU_DTYPE),
            per_chunk(2 * HEAD_DIM, MXU_DTYPE),
            per_chunk(HEAD_DIM, jnp.float32),
            per_chunk(c, MXU_DTYPE),
            per_chunk(HEAD_DIM, jnp.float32),
            pltpu.VMEM((heads, n_chunks, SUBLANES, HEAD_DIM), jnp.float32),
            pltpu.VMEM((heads, n_chunks, 2 * c, HEAD_DIM), MXU_DTYPE),
            per_chunk(HEAD_DIM, jnp.float32),
            per_chunk(HEAD_DIM, jnp.float32),
            per_chunk(HEAD_DIM, jnp.float32),
        ],
        compiler_params=_cparams("parallel", "parallel"),
        name="gated_deltanet",
    )(qkv, qkv, qkv, zact, *gates, gain)


def kernel(x, ffn1_norm, ffn1_w_gate, ffn1_w_up, ffn1_w_down, mix_norm, w_in, conv_w, a_log, dt_bias,
           dn_norm, w_out, ffn2_norm, ffn2_w_gate, ffn2_w_up, ffn2_w_down, final_norm):
    batch, seq, d_model = x.shape
    n_heads = a_log.shape[0]
    d_head_group = n_heads * HEAD_DIM
    h = x.reshape(batch * seq, d_model)

    h, xn = ffn(h, ffn1_norm, ffn1_w_gate, ffn1_w_up, ffn1_w_down, mix_norm, True, MXU_DTYPE)

    wt = w_in.T
    qkv_attn = matmul_nt(xn, wt, 0, 3 * d_head_group, MXU_DTYPE, lead_cols=d_head_group, lead_scale=ATTN_Q_SCALE)
    qkv_dn = proj_conv_qkv(xn, wt, conv_w, 3 * d_head_group, seq)
    zact = matmul_nt(xn, wt, 6 * d_head_group, d_head_group, jnp.float32, act=_silu)
    n_main = 7 * d_head_group
    rows_pad = lambda t: jnp.pad(t.astype(jnp.float32), ((0, HEAD_DIM - n_heads), (0, 0)))
    lane_pad = lambda t: jnp.pad(t.astype(jnp.float32).reshape(1, n_heads), ((0, 0), (0, HEAD_DIM - n_heads)))
    gates = dn_gates(xn, rows_pad(wt[n_main:n_main + n_heads]), rows_pad(wt[n_main + n_heads:]),
                     lane_pad(a_log), lane_pad(dt_bias))
    attn = dilated_attention(qkv_attn, batch, seq, n_heads)
    dn = gated_deltanet(qkv_dn, zact, gates, dn_norm.astype(jnp.float32).reshape(1, HEAD_DIM),
                        batch, seq, n_heads)
    h = matmul2_residual(attn, dn, w_out, h)

    out = ffn(h, ffn2_norm, ffn2_w_gate, ffn2_w_up, ffn2_w_down, final_norm, False, jnp.float32)
    return out.reshape(batch, seq, d_model)
```

```python
import functools
import math

import jax
import jax.numpy as jnp
from jax import lax
from jax.experimental import pallas as pl
from jax.experimental.pallas import tpu as pltpu

EPS = 1e-6
HEAD_DIM = 128
DILATED_CONFIGS = ((128, 1), (512, 4), (2048, 16))
CONV_WIDTH = 4
ATTN_HEADS_PER_STEP = 2
DN_CHUNK = 128
DN_HEADS_PER_STEP = 2
DN_CHUNKS_PER_ITER = 8
MASKED_BIAS = -1e30
ATTN_Q_SCALE = HEAD_DIM ** -0.5 * math.log2(math.e)
SUBLANES = 8

VMEM_LIMIT_BYTES = 56 * 1024 * 1024
FFN_VMEM_LIMIT_BYTES = 60 * 1024 * 1024
MXU_DTYPE = jnp.bfloat16


def _cparams(*sem):
    return pltpu.CompilerParams(dimension_semantics=sem, vmem_limit_bytes=VMEM_LIMIT_BYTES)


def _dot(a, b):
    return jnp.dot(a.astype(MXU_DTYPE), b.astype(MXU_DTYPE), preferred_element_type=jnp.float32)


def _dot_nt(a, b):
    return lax.dot_general(a.astype(MXU_DTYPE), b.astype(MXU_DTYPE), (((1,), (1,)), ((), ())),
                           preferred_element_type=jnp.float32)


def _silu(x):
    hx = 0.5 * x
    return hx + hx * jnp.tanh(hx)


PROJ_CONV_ROW_CHUNK = 512
FFN_ROW_CHUNK = 32
FFN_NORM_PIECES = 4
FFN_IN_CHUNKS = 8
FFN_K_CHUNK = 1024
FFN_N_CHUNK = 512


def _rmsnorm_rows(src_ref, w_ref, dst_ref, row0, n_rows):
    w = w_ref[...]
    step_rows = FFN_ROW_CHUNK * FFN_NORM_PIECES

    def body(r, carry):
        pieces = [pl.ds(pl.multiple_of(row0 + r * step_rows + p * FFN_ROW_CHUNK, FFN_ROW_CHUNK), FFN_ROW_CHUNK)
                  for p in range(FFN_NORM_PIECES)]
        sq = [src_ref[rows, :] for rows in pieces]
        inv = [lax.rsqrt(jnp.mean(x * x, axis=-1, keepdims=True) + EPS) for x in sq]
        for rows, scale in zip(pieces, inv):
            dst_ref[rows, :] = ((src_ref[rows, :] * scale) * w).astype(dst_ref.dtype)
        return carry
    lax.fori_loop(0, n_rows // step_rows, body, 0)


def _ffn_kernel(h_hbm, nw_ref, wg_ref, wu_ref, wd_ref, onw_ref, *rest, emit_res):
    if emit_res:
        res_hbm, norm_hbm, acc, xn_s, sem_in, sem_out = rest
    else:
        norm_hbm, acc, xn_s, sem_in, sem_out = rest
    i, j = pl.program_id(0), pl.program_id(1)
    n_i, n_j = pl.num_programs(0), pl.num_programs(1)
    bm, d = acc.shape
    rows_in = bm // FFN_IN_CHUNKS
    tile = lambda ref, t: ref.at[pl.ds(pl.multiple_of(t * bm, bm), bm), :]

    def in_copy(t, c):
        part = pl.ds(pl.multiple_of(c * rows_in, rows_in), rows_in)
        return pltpu.make_async_copy(tile(h_hbm, t).at[part, :], acc.at[part, :], sem_in.at[c])

    res_copy = lambda t: pltpu.make_async_copy(acc, tile(res_hbm, t), sem_out.at[0])
    norm_copy = lambda t: pltpu.make_async_copy(xn_s if emit_res else acc, tile(norm_hbm, t), sem_out.at[1])

    @pl.when(j == 0)
    def _():
        @pl.when(i > 0)
        def _():
            (res_copy if emit_res else norm_copy)(i - 1).wait()

        for c in range(FFN_IN_CHUNKS):
            in_copy(i, c).start()
        if emit_res:
            @pl.when(i > 0)
            def _():
                norm_copy(i - 1).wait()

        def land(c, carry):
            in_copy(i, c).wait()
            _rmsnorm_rows(acc, nw_ref, xn_s, c * rows_in, rows_in)
            return carry
        lax.fori_loop(0, FFN_IN_CHUNKS, land, 0)

    g = u = None
    for kc in range(d // FFN_K_CHUNK):
        ks = slice(kc * FFN_K_CHUNK, (kc + 1) * FFN_K_CHUNK)
        xk = xn_s[:, ks]
        gk, uk = _dot(xk, wg_ref[ks, :]), _dot(xk, wu_ref[ks, :])
        g, u = (gk, uk) if g is None else (g + gk, u + uk)
    act = ((0.5 * _silu(g)) * u).astype(MXU_DTYPE)
    for nc in range(d // FFN_N_CHUNK):
        ns = slice(nc * FFN_N_CHUNK, (nc + 1) * FFN_N_CHUNK)
        acc[:, ns] += _dot(act, wd_ref[:, ns])

    @pl.when(j == n_j - 1)
    def _():
        if emit_res:
            res_copy(i).start()
            _rmsnorm_rows(acc, onw_ref, xn_s, 0, bm)
        else:
            _rmsnorm_rows(acc, onw_ref, acc, 0, bm)
        norm_copy(i).start()

        @pl.when(i == n_i - 1)
        def _():
            if emit_res:
                res_copy(i).wait()
            norm_copy(i).wait()


def ffn(h, norm_w, w_gate, w_up, w_down, out_norm_w, emit_res, norm_dtype, bm=1024, bf=256):
    t, d = h.shape
    f = w_gate.shape[1]
    assert t % bm == 0 and f % bf == 0 and d % FFN_K_CHUNK == 0 and d % FFN_N_CHUNK == 0
    assert bm % (FFN_IN_CHUNKS * FFN_ROW_CHUNK * FFN_NORM_PIECES) == 0
    assert emit_res or norm_dtype == jnp.float32
    any_spec = pl.BlockSpec(memory_space=pl.ANY)
    row_spec = pl.BlockSpec((1, d), lambda i, j: (0, 0))
    norm_shape = jax.ShapeDtypeStruct((t, d), norm_dtype)
    return pl.pallas_call(
        functools.partial(_ffn_kernel, emit_res=emit_res),
        grid=(t // bm, f // bf),
        in_specs=[any_spec, row_spec,
                  pl.BlockSpec((d, bf), lambda i, j: (0, j)),
                  pl.BlockSpec((d, bf), lambda i, j: (0, j)),
                  pl.BlockSpec((bf, d), lambda i, j: (j, 0)),
                  row_spec],
        out_specs=[any_spec, any_spec] if emit_res else any_spec,
        out_shape=[jax.ShapeDtypeStruct((t, d), jnp.float32), norm_shape] if emit_res else norm_shape,
        scratch_shapes=[pltpu.VMEM((bm, d), jnp.float32), pltpu.VMEM((bm, d), MXU_DTYPE),
                        pltpu.SemaphoreType.DMA((FFN_IN_CHUNKS,)), pltpu.SemaphoreType.DMA((2,))],
        compiler_params=pltpu.CompilerParams(dimension_semantics=("arbitrary", "arbitrary"),
                                             vmem_limit_bytes=FFN_VMEM_LIMIT_BYTES),
        name="ffn",
    )(h, norm_w.reshape(1, d), w_gate, w_up, w_down, out_norm_w.reshape(1, d))


def _mm_kernel(a_ref, w_ref, o_ref, *, act, lead_blocks, lead_scale):
    acc = _dot_nt(a_ref[...], w_ref[...])
    if lead_blocks:
        acc = acc * jnp.where(pl.program_id(1) < lead_blocks, lead_scale, 1.0)
    o_ref[...] = (act(acc) if act else acc).astype(o_ref.dtype)


def matmul_nt(a, wt, row0, n, out_dtype, act=None, lead_cols=0, lead_scale=1.0, bm=1024, bn=512):
    t, k = a.shape
    assert row0 % bn == 0 and n % bn == 0 and lead_cols % bn == 0
    return pl.pallas_call(
        functools.partial(_mm_kernel, act=act, lead_blocks=lead_cols // bn, lead_scale=lead_scale),
        grid=(t // bm, n // bn),
        in_specs=[pl.BlockSpec((bm, k), lambda i, j: (i, 0)),
                  pl.BlockSpec((bn, k), lambda i, j: (row0 // bn + j, 0))],
        out_specs=pl.BlockSpec((bm, bn), lambda i, j: (i, j)),
        out_shape=jax.ShapeDtypeStruct((t, n), out_dtype),
        compiler_params=_cparams("parallel", "parallel"),
        name="matmul_nt",
    )(a, wt)


def _proj_conv_kernel(a_ref, w_ref, cw_ref, o_ref, acc_s, *, q_blocks, k_blocks):
    seq, bn = o_ref.shape
    chunk = min(PROJ_CONV_ROW_CHUNK, seq)
    w = w_ref[...].astype(MXU_DTYPE)
    cw = cw_ref[...]
    j = pl.program_id(1)
    is_v = j >= q_blocks + k_blocks
    qk_scale = jnp.where(j < q_blocks, HEAD_DIM ** -0.5, 1.0)
    row = lax.broadcasted_iota(jnp.int32, (SUBLANES, bn), 0)
    acc_s[:SUBLANES, :] = jnp.zeros((SUBLANES, bn), jnp.float32)
    for r in range(seq // chunk):
        acc_s[SUBLANES + r * chunk:SUBLANES + (r + 1) * chunk, :] = _dot_nt(a_ref[r * chunk:(r + 1) * chunk, :], w)
    for r in range(seq // chunk):
        y = None
        for i in range(CONV_WIDTH):
            start = SUBLANES + r * chunk - (CONV_WIDTH - 1 - i)
            term = cw[i:i + 1, :] * acc_s[start:start + chunk, :]
            y = term if y is None else y + term
        y = _silu(y)
        heads = []
        for g in range(bn // HEAD_DIM):
            yg = y[:, g * HEAD_DIM:(g + 1) * HEAD_DIM]
            inv = lax.rsqrt(jnp.sum(yg * yg, axis=-1, keepdims=True) + EPS) * qk_scale
            heads.append(yg * jnp.where(is_v, 1.0, inv))
        o_ref[r * chunk:(r + 1) * chunk, :] = jnp.concatenate(heads, axis=1)


def proj_conv_qkv(a, wt, conv_w, row0, seq, bn=256):
    t, k = a.shape
    n = conv_w.shape[1]
    width = n // 3
    assert row0 % bn == 0 and width % bn == 0 and bn % HEAD_DIM == 0
    return pl.pallas_call(
        functools.partial(_proj_conv_kernel, q_blocks=width // bn, k_blocks=width // bn),
        grid=(t // seq, n // bn),
        in_specs=[pl.BlockSpec((seq, k), lambda i, j: (i, 0)),
                  pl.BlockSpec((bn, k), lambda i, j: (row0 // bn + j, 0)),
                  pl.BlockSpec((CONV_WIDTH, bn), lambda i, j: (0, j))],
        out_specs=pl.BlockSpec((seq, bn), lambda i, j: (i, j)),
        out_shape=jax.ShapeDtypeStruct((t, n), jnp.float32),
        scratch_shapes=[pltpu.VMEM((SUBLANES + seq, bn), jnp.float32)],
        compiler_params=_cparams("parallel", "parallel"),
        name="proj_conv_qkv",
    )(a, wt, conv_w)


def _mm2_res_kernel(a1_ref, a2_ref, w1_ref, w2_ref, r_ref, o_ref):
    o_ref[...] = r_ref[...] + (_dot(a1_ref[...], w1_ref[...]) + _dot(a2_ref[...], w2_ref[...]))


def matmul2_residual(a1, a2, w, res, bm=512, bn=1024):
    t, k1 = a1.shape
    assert a2.shape[1] == k1 and w.shape[0] == 2 * k1
    n = w.shape[1]
    return pl.pallas_call(
        _mm2_res_kernel,
        grid=(n // bn, t // bm),
        in_specs=[pl.BlockSpec((bm, k1), lambda j, i: (i, 0)),
                  pl.BlockSpec((bm, k1), lambda j, i: (i, 0)),
                  pl.BlockSpec((k1, bn), lambda j, i: (0, j)),
                  pl.BlockSpec((k1, bn), lambda j, i: (1, j)),
                  pl.BlockSpec((bm, bn), lambda j, i: (i, j))],
        out_specs=pl.BlockSpec((bm, bn), lambda j, i: (i, j)),
        out_shape=jax.ShapeDtypeStruct((t, n), jnp.float32),
        compiler_params=_cparams("parallel", "parallel"),
        name="matmul2_residual",
    )(a1, a2, w, w, res)


def _attn_kernel(q_ref, k_ref, v_ref, o_ref, bias_ref, *, seq, tq):
    @pl.when((pl.program_id(0) == 0) & (pl.program_id(1) == 0))
    def _():
        i = lax.broadcasted_iota(jnp.int32, (tq, seq), 0)
        jj = lax.broadcasted_iota(jnp.int32, (tq, seq), 1)
        dist = i - (jj - (seq - tq))
        count = jnp.zeros((tq, seq), jnp.int32)
        for window, d in DILATED_CONFIGS:
            hit = (dist >= 0) & (dist <= (window // d) * d) & ((dist & (d - 1)) == 0)
            count = count + hit.astype(jnp.int32)
        bias_ref[...] = jnp.where(
            count == 3, math.log2(3.0),
            jnp.where(count == 2, 1.0, jnp.where(count == 1, 0.0, MASKED_BIAS)))

    n_tiles = seq // tq
    cols = [slice(e * HEAD_DIM, (e + 1) * HEAD_DIM) for e in range(ATTN_HEADS_PER_STEP)]
    ks = [k_ref[:, c].astype(MXU_DTYPE) for c in cols]
    vs = [v_ref[:, c].astype(MXU_DTYPE) for c in cols]

    def scores(qi):
        kend = (qi + 1) * tq
        return [_dot_nt(q_ref[qi * tq:kend, c], k[:kend]) + bias_ref[:, seq - kend:] for c, k in zip(cols, ks)]

    ss = scores(0)
    for qi in range(n_tiles):
        kend = (qi + 1) * tq
        ss_next = scores(qi + 1) if qi + 1 < n_tiles else None
        ms = [jnp.max(s, axis=-1, keepdims=True) for s in ss]
        ps = [jnp.exp2(s - m) for s, m in zip(ss, ms)]
        dens = [jnp.sum(p, axis=-1, keepdims=True) for p in ps]
        outs = [_dot(p, v[:kend]) for p, v in zip(ps, vs)]
        for c, o, den in zip(cols, outs, dens):
            o_ref[qi * tq:kend, c] = (o / den).astype(o_ref.dtype)
        ss = ss_next


def dilated_attention(proj, batch, seq, n_heads, tq=256):
    tq = min(tq, seq)
    for window, d in DILATED_CONFIGS:
        assert d & (d - 1) == 0
    heads = ATTN_HEADS_PER_STEP
    assert n_heads % heads == 0
    wide = heads * HEAD_DIM
    blk = lambda off: pl.BlockSpec((seq, wide), lambda b, h: (b, off // heads + h))
    return pl.pallas_call(
        functools.partial(_attn_kernel, seq=seq, tq=tq),
        grid=(batch, n_heads // heads),
        in_specs=[blk(0), blk(n_heads), blk(2 * n_heads)],
        out_specs=pl.BlockSpec((seq, wide), lambda b, h: (b, h)),
        out_shape=jax.ShapeDtypeStruct((batch * seq, n_heads * HEAD_DIM), MXU_DTYPE),
        scratch_shapes=[pltpu.VMEM((tq, seq), jnp.float32)],
        compiler_params=_cparams("arbitrary", "arbitrary"),
        name="dilated_attention",
    )(proj, proj, proj)


def _dn_gates_kernel(x_ref, wb_ref, wa_ref, alog_ref, dtb_ref, beta_ref, gc_ref):
    c = DN_CHUNK
    x = x_ref[...]
    beta_ref[...] = jax.nn.sigmoid(_dot_nt(x, wb_ref[...]))
    z = _dot_nt(x, wa_ref[...]) + dtb_ref[...]
    softplus = jnp.maximum(z, 0.0) + jnp.log1p(jnp.exp(-jnp.abs(z)))
    gc = -jnp.exp(alog_ref[...]) * softplus
    row = lax.broadcasted_iota(jnp.int32, gc.shape, 0)
    s = 1
    while s < c:
        gc = gc + jnp.where((row & (c - 1)) >= s, pltpu.roll(gc, s, axis=0), 0.0)
        s *= 2
    gc_ref[...] = gc


def dn_gates(xn, wt_beta, wt_decay, a_row, dt_row, bm=1024):
    t, d = xn.shape
    out = jax.ShapeDtypeStruct((t, HEAD_DIM), jnp.float32)
    tok = pl.BlockSpec((bm, HEAD_DIM), lambda i: (i, 0))
    wspec = pl.BlockSpec((HEAD_DIM, d), lambda i: (0, 0))
    row_spec = pl.BlockSpec((1, HEAD_DIM), lambda i: (0, 0))
    return pl.pallas_call(
        _dn_gates_kernel,
        grid=(t // bm,),
        in_specs=[pl.BlockSpec((bm, d), lambda i: (i, 0)), wspec, wspec, row_spec, row_spec],
        out_specs=[tok, tok],
        out_shape=[out, out],
        compiler_params=_cparams("parallel"),
        name="dn_gates",
    )(xn, wt_beta, wt_decay, a_row, dt_row)


def _lane_column(x, lane, col):
    picked = jnp.sum(jnp.where(lane == col, x, 0.0), axis=-1, keepdims=True)
    return jnp.broadcast_to(picked, x.shape)


def _unit_lower_inverses(nmats, ci, cj):
    c = nmats[0].shape[0]
    eye = (ci == cj).astype(jnp.float32)
    same8 = (ci >> 3) == (cj >> 3)
    ms = [-jnp.where(same8, nmat, 0.0) for nmat in nmats]
    tinvs = [eye + m for m in ms]
    ms = [_dot(m, m) for m in ms]
    yield
    xs = [_dot(m, jnp.concatenate([m, tinv], axis=1)) for m, tinv in zip(ms, tinvs)]
    yield
    tinvs = [tinv + x[:, c:] for tinv, x in zip(tinvs, xs)]
    ys = [_dot(x[:, :c], tinv) for x, tinv in zip(xs, tinvs)]
    yield
    tinvs = [tinv + y for tinv, y in zip(tinvs, ys)]
    shift = 3
    while (1 << shift) < c:
        pick = ((ci >> (shift + 1)) == (cj >> (shift + 1))) & ((ci >> shift) != (cj >> shift))
        ys = [_dot(jnp.where(pick, nmat, 0.0), tinv) for nmat, tinv in zip(nmats, tinvs)]
        yield
        ys = [_dot(tinv, y) for tinv, y in zip(tinvs, ys)]
        yield
        tinvs = [tinv - y for tinv, y in zip(tinvs, ys)]
        shift += 1
    return tinvs


def _interleave(first, second):
    results = [None, None]
    live = [first, second]
    while any(g is not None for g in live):
        for idx, g in enumerate(live):
            if g is None:
                continue
            try:
                next(g)
            except StopIteration as stop:
                results[idx] = stop.value
                live[idx] = None
    return results


def _deltanet_kernel(q_ref, k_ref, v_ref, z_ref, beta_ref, gc_ref, gain_ref, o_ref,
                     q_s, k_s, kb_s, rhs_s, qd_s, kdt_s, gc_s, egl_s, mc_s, b_s, d_s, o_s, *, seq):
    c = DN_CHUNK
    hd = HEAD_DIM
    n_chunks = seq // c
    group = DN_CHUNKS_PER_ITER
    heads = DN_HEADS_PER_STEP
    lane = lax.broadcasted_iota(jnp.int32, (seq, hd), 1)

    for e in range(heads):
        cols = slice(e * hd, (e + 1) * hd)
        head = pl.program_id(1) * heads + e
        beta = _lane_column(beta_ref[...], lane, head)
        gc = _lane_column(gc_ref[...], lane, head).reshape(n_chunks, c, hd)
        eg = jnp.exp(gc)
        kd = jnp.exp(gc[:, c - 1:c, :] - gc)
        q = q_ref[:, cols]
        k = k_ref[:, cols]
        kb = k * beta
        kdt = (k * kd.reshape(seq, hd)).T
        eg2 = eg.reshape(seq, hd)
        q_s[e] = q.astype(q_s.dtype).reshape(n_chunks, c, hd)
        k_s[e] = k.astype(k_s.dtype).reshape(n_chunks, c, hd)
        kb_s[e] = kb.astype(kb_s.dtype).reshape(n_chunks, c, hd)
        rhs_s[e] = jnp.concatenate([kb * eg2, v_ref[:, cols] * beta], axis=1
                                   ).astype(rhs_s.dtype).reshape(n_chunks, c, 2 * hd)
        qd_s[e] = (q * eg2).reshape(n_chunks, c, hd)
        gc_s[e] = gc
        egl_s[e] = eg[:, c - SUBLANES:, :]
        for n in range(n_chunks):
            kdt_s[e, n] = kdt[:, n * c:(n + 1) * c].astype(kdt_s.dtype)

    ci = lax.broadcasted_iota(jnp.int32, (c, c), 0)
    cj = lax.broadcasted_iota(jnp.int32, (c, c), 1)

    def prep_stages(it):
        chains = [(e, it * group + g) for g in range(group) for e in range(heads)]
        ks = [k_s[e, n] for e, n in chains]
        kbs = [kb_s[e, n] for e, n in chains]
        qs = [q_s[e, n] for e, n in chains]
        gcs = [gc_s[e, n] for e, n in chains]
        rhss = [rhs_s[e, n] for e, n in chains]
        qds = [qd_s[e, n] for e, n in chains]
        kdts = [kdt_s[e, n] for e, n in chains]
        yield
        kks = [_dot_nt(kb, k) for kb, k in zip(kbs, ks)]
        qks = [_dot_nt(q, k) for q, k in zip(qs, ks)]
        yield
        decays = [jnp.exp(jnp.where(ci >= cj, gc - gc.T, -jnp.inf)) for gc in gcs]
        nmats = [jnp.where(ci > cj, kk * decay, 0.0) for kk, decay in zip(kks, decays)]
        tinvs = yield from _unit_lower_inverses(nmats, ci, cj)
        sols = [_dot(tinv, rhs) for tinv, rhs in zip(tinvs, rhss)]
        yield
        lhs = [jnp.concatenate([kdt, (qk * decay).astype(MXU_DTYPE)], axis=0)
               for kdt, qk, decay in zip(kdts, qks, decays)]
        xys = [_dot(l, sol) for l, sol in zip(lhs, sols)]
        yield
        return [(e, n, jnp.concatenate([-xy[:c, :hd], qd - xy[c:, :hd]], axis=0),
                 xy[:c, hd:], xy[c:, hd:]) for (e, n), xy, qd in zip(chains, xys, qds)]

    def store_prepped(prepped):
        for e, n, mc, bmat, dmat in prepped:
            mc_s[e, n] = mc.astype(mc_s.dtype)
            b_s[e, n] = bmat
            d_s[e, n] = dmat

    def scan_stages(it, states):
        r = range(heads)
        loaded = []
        for g in range(group):
            n = it * group + g
            loaded.append(([mc_s[e, n] for e in r], [b_s[e, n] for e in r], [d_s[e, n] for e in r],
                           [egl_s[e, n][SUBLANES - 1:SUBLANES, :] for e in r]))
        yield
        outs = []
        for mcs, bs, ds, egls in loaded:
            prods = [_dot(mcs[e], states[e]) for e in r]
            yield
            outs.append([prods[e][c:, :] + ds[e] for e in r])
            states = tuple(states[e] * egls[e] + prods[e][:c, :] + bs[e] for e in r)
        return states, outs

    def store_outs(it, outs):
        for g, chunk_outs in enumerate(outs):
            for e in range(heads):
                o_s[e, it * group + g] = chunk_outs[e]

    def run(gen):
        return _interleave(gen, None)[0]

    n_groups = n_chunks // group
    store_prepped(run(prep_stages(0)))

    def body(it, states):
        prepped, (states, outs) = _interleave(prep_stages(it), scan_stages(it - 1, states))
        store_prepped(prepped)
        store_outs(it - 1, outs)
        return states

    states = lax.fori_loop(1, n_groups, body, tuple(jnp.zeros((hd, hd), jnp.float32) for _ in range(heads)))
    _, outs = run(scan_stages(n_groups - 1, states))
    store_outs(n_groups - 1, outs)

    for e in range(heads):
        cols = slice(e * hd, (e + 1) * hd)
        o = o_s[e].reshape(seq, hd)
        o = o * lax.rsqrt(jnp.mean(o * o, axis=-1, keepdims=True) + EPS) * gain_ref[...]
        o_ref[:, cols] = (o * z_ref[:, cols]).astype(o_ref.dtype)


def gated_deltanet(qkv, zact, gates, gain, batch, seq, n_heads):
    heads = DN_HEADS_PER_STEP
    c = DN_CHUNK
    n_chunks = seq // c
    assert n_heads % heads == 0 and n_chunks % DN_CHUNKS_PER_ITER == 0
    wide = heads * HEAD_DIM
    blk = lambda off: pl.BlockSpec((seq, wide), lambda b, h: (b, off // heads + h))
    gate_spec = pl.BlockSpec((seq, HEAD_DIM), lambda b, h: (b, 0))
    per_chunk = lambda width, dtype: pltpu.VMEM((heads, n_chunks, c, width), dtype)
    return pl.pallas_call(
        functools.partial(_deltanet_kernel, seq=seq),
        grid=(batch, n_heads // heads),
        in_specs=[blk(0), blk(n_heads), blk(2 * n_heads), blk(0),
                  gate_spec, gate_spec,
                  pl.BlockSpec((1, HEAD_DIM), lambda b, h: (0, 0))],
        out_specs=pl.BlockSpec((seq, wide), lambda b, h: (b, h)),
        out_shape=jax.ShapeDtypeStruct((batch * seq, n_heads * HEAD_DIM), MXU_DTYPE),
        scratch_shapes=[
            per_chunk(HEAD_DIM, MXU_DTYPE),
            per_chunk(HEAD_DIM, MXU_DTYPE),
            per_chunk(HEAD_DIM, MXU_DTYPE),
            per_chunk(2 * HEAD_DIM, MXU_DTYPE),
            per_chunk(HEAD_DIM, jnp.float32),
            per_chunk(c, MXU_DTYPE),
            per_chunk(HEAD_DIM, jnp.float32),
            pltpu.VMEM((heads, n_chunks, SUBLANES, HEAD_DIM), jnp.float32),
            pltpu.VMEM((heads, n_chunks, 2 * c, HEAD_DIM), MXU_DTYPE),
            per_chunk(HEAD_DIM, jnp.float32),
            per_chunk(HEAD_DIM, jnp.float32),
            per_chunk(HEAD_DIM, jnp.float32),
        ],
        compiler_params=_cparams("parallel", "parallel"),
        name="gated_deltanet",
    )(qkv, qkv, qkv, zact, *gates, gain)


def kernel(x, ffn1_norm, ffn1_w_gate, ffn1_w_up, ffn1_w_down, mix_norm, w_in, conv_w, a_log, dt_bias,
           dn_norm, w_out, ffn2_norm, ffn2_w_gate, ffn2_w_up, ffn2_w_down, final_norm):
    batch, seq, d_model = x.shape
    n_heads = a_log.shape[0]
    d_head_group = n_heads * HEAD_DIM
    h = x.reshape(batch * seq, d_model)

    h, xn = ffn(h, ffn1_norm, ffn1_w_gate, ffn1_w_up, ffn1_w_down, mix_norm, True, MXU_DTYPE)

    wt = w_in.T
    qkv_attn = matmul_nt(xn, wt, 0, 3 * d_head_group, MXU_DTYPE, lead_cols=d_head_group, lead_scale=ATTN_Q_SCALE)
    qkv_dn = proj_conv_qkv(xn, wt, conv_w, 3 * d_head_group, seq)
    zact = matmul_nt(xn, wt, 6 * d_head_group, d_head_group, jnp.float32, act=_silu)
    n_main = 7 * d_head_group
    rows_pad = lambda t: jnp.pad(t.astype(jnp.float32), ((0, HEAD_DIM - n_heads), (0, 0)))
    lane_pad = lambda t: jnp.pad(t.astype(jnp.float32).reshape(1, n_heads), ((0, 0), (0, HEAD_DIM - n_heads)))
    gates = dn_gates(xn, rows_pad(wt[n_main:n_main + n_heads]), rows_pad(wt[n_main + n_heads:]),
                     lane_pad(a_log), lane_pad(dt_bias))
    attn = dilated_attention(qkv_attn, batch, seq, n_heads)
    dn = gated_deltanet(qkv_dn, zact, gates, dn_norm.astype(jnp.float32).reshape(1, HEAD_DIM),
                        batch, seq, n_heads)
    h = matmul2_residual(attn, dn, w_out, h)

    out = ffn(h, ffn2_norm, ffn2_w_gate, ffn2_w_up, ffn2_w_down, final_norm, False, jnp.float32)
    return out.reshape(batch, seq, d_model)
```
